```python
import jax, jax.numpy as jnp
from jax import lax
import numpy as np

D_MODEL = 2048
BATCH = 4
SEQ = 2048
DEPTH = 4

HEAD_DIM = 128
N_Q_HEADS = 12
N_KV_HEADS = 4
GROUP = N_Q_HEADS // N_KV_HEADS
N_MEM_HEADS = 4
MEM_LEN = 256
WINDOW = 128
SW_BLOCK = 128
MOBA_BLOCK = 256
MOBA_TOPK = 3
MOBA_Q_CHUNK = 16
D_FF = 4 * D_MODEL
N_A_LAYERS = DEPTH // 2
N_B_LAYERS = DEPTH - N_A_LAYERS
ALPHA = (2 * DEPTH) ** 0.25
BETA = (8 * DEPTH) ** -0.25
LN_EPS = 1e-5
NEG_INF = -1e30
Q_W = N_Q_HEADS * HEAD_DIM
KV_W = N_KV_HEADS * HEAD_DIM
MQ_W = N_MEM_HEADS * HEAD_DIM
MIX_W = Q_W + MQ_W
IN_A_W = Q_W + 2 * KV_W + MQ_W

kernel_name = "yoco_swa_sink_moba_memory_deepnorm"


def alibi_slopes():
    return 2.0 ** (-8.0 * jnp.arange(1, N_Q_HEADS + 1, dtype=jnp.float32) / N_Q_HEADS)


def layer_norm(x, g, b):
    xf = x.astype(jnp.float32)
    mu = jnp.mean(xf, axis=-1, keepdims=True)
    var = jnp.mean(jnp.square(xf - mu), axis=-1, keepdims=True)
    return ((xf - mu) * lax.rsqrt(var + LN_EPS) * g.astype(jnp.float32)
            + b.astype(jnp.float32)).astype(x.dtype)


def sliding_window_attention(q, k, v, sinks, slopes):
    B, T = q.shape[0], q.shape[1]
    nb = T // SW_BLOCK
    scale = HEAD_DIM ** -0.5
    qb = q.reshape(B, nb, SW_BLOCK, N_KV_HEADS, GROUP, HEAD_DIM)
    kb = k.reshape(B, nb, SW_BLOCK, N_KV_HEADS, HEAD_DIM)
    vb = v.reshape(B, nb, SW_BLOCK, N_KV_HEADS, HEAD_DIM)
    prev = lambda a: jnp.concatenate([jnp.zeros_like(a[:, :1]), a[:, :-1]], axis=1)
    kw = jnp.concatenate([prev(kb), kb], axis=2)
    vw = jnp.concatenate([prev(vb), vb], axis=2)
    logits = jnp.einsum('bnqkgd,bnskd->bnkgqs', qb, kw).astype(jnp.float32) * scale
    qi = jnp.arange(SW_BLOCK)[:, None] + SW_BLOCK
    sj = jnp.arange(2 * SW_BLOCK)[None, :]
    dist = qi - sj
    blk = jnp.arange(nb)[:, None, None]
    valid = (dist >= 0) & (dist < WINDOW) & ((blk > 0) | (sj >= SW_BLOCK))
    slopes_kg = slopes.reshape(N_KV_HEADS, GROUP)[:, :, None, None]
    logits = logits - slopes_kg * dist.astype(jnp.float32)
    logits = jnp.where(valid[None, :, None, None], logits, NEG_INF)
    sink = sinks.astype(jnp.float32).reshape(N_KV_HEADS, GROUP)[:, :, None, None]
    m = jnp.maximum(jnp.max(logits, axis=-1, keepdims=True), sink)
    p = jnp.exp(logits - m)
    denom = jnp.sum(p, axis=-1, keepdims=True) + jnp.exp(sink - m)
    out = jnp.einsum('bnkgqs,bnskd->bnqkgd', (p / denom).astype(v.dtype), vw)
    return out.reshape(B, T, Q_W)


def moba_attention(q, k_blocks, v_blocks, k_means, slopes):
    B, T = q.shape[0], q.shape[1]
    nb = k_blocks.shape[2]
    k_sel = min(MOBA_TOPK, nb)
    n_chunks = T // MOBA_Q_CHUNK
    scale = HEAD_DIM ** -0.5
    slopes_kg = slopes.reshape(N_KV_HEADS, GROUP)
    b_idx = jnp.arange(B)[:, None, None, None, None]
    h_idx = jnp.arange(N_KV_HEADS)[None, None, :, None, None]
    q_chunks = q.reshape(B, n_chunks, MOBA_Q_CHUNK, N_KV_HEADS, GROUP, HEAD_DIM)
    q_chunks = jnp.moveaxis(q_chunks, 1, 0)

    def chunk(args):
        c, qc = args
        start = c * MOBA_Q_CHUNK
        t = start + jnp.arange(MOBA_Q_CHUNK)
        j = start // MOBA_BLOCK
        gate = jnp.einsum('bqkgd,bknd->bqkgn', qc, k_means).astype(jnp.float32)
        gate = jnp.where(jnp.arange(nb) < j, gate, NEG_INF)
        _, sel = lax.top_k(gate, k_sel)
        slot_valid = jnp.arange(k_sel) < j
        kg = k_blocks[b_idx, h_idx, sel]
        vg = v_blocks[b_idx, h_idx, sel]
        lg = jnp.einsum('bqkgd,bqkgjsd->bqkgjs', qc, kg).astype(jnp.float32) * scale
        kpos = sel[..., None] * MOBA_BLOCK + jnp.arange(MOBA_BLOCK)
        dist = (t[None, :, None, None, None, None] - kpos).astype(jnp.float32)
        lg = lg - slopes_kg[None, None, :, :, None, None] * dist
        lg = jnp.where(slot_valid[:, None], lg, NEG_INF)
        lg = lg.reshape(B, MOBA_Q_CHUNK, N_KV_HEADS, GROUP, k_sel * MOBA_BLOCK)
        ko = lax.dynamic_index_in_dim(k_blocks, j, axis=2, keepdims=False)
        vo = lax.dynamic_index_in_dim(v_blocks, j, axis=2, keepdims=False)
        lo = jnp.einsum('bqkgd,bksd->bqkgs', qc, ko).astype(jnp.float32) * scale
        dist_o = (t[:, None] - (j * MOBA_BLOCK + jnp.arange(MOBA_BLOCK))[None, :])
        dist_o = dist_o[None, :, None, None, :]
        lo = lo - slopes_kg[None, None, :, :, None] * dist_o.astype(jnp.float32)
        lo = jnp.where(dist_o >= 0, lo, NEG_INF)
        p = jax.nn.softmax(jnp.concatenate([lg, lo], axis=-1), axis=-1)
        pg = p[..., :k_sel * MOBA_BLOCK].reshape(B, MOBA_Q_CHUNK, N_KV_HEADS, GROUP, k_sel, MOBA_BLOCK)
        po = p[..., k_sel * MOBA_BLOCK:]
        return (jnp.einsum('bqkgjs,bqkgjsd->bqkgd', pg.astype(vg.dtype), vg)
                + jnp.einsum('bqkgs,bksd->bqkgd', po.astype(vo.dtype), vo))

    out = lax.map(chunk, (jnp.arange(n_chunks), q_chunks))
    return jnp.moveaxis(out, 0, 1).reshape(B, T, Q_W)


def shared_moba_kv(h, w_kv_shared):
    B, T = h.shape[0], h.shape[1]
    kv = h @ w_kv_shared
    k = kv[..., :KV_W].reshape(B, T, N_KV_HEADS, HEAD_DIM)
    v = kv[..., KV_W:].reshape(B, T, N_KV_HEADS, HEAD_DIM)
    nb = -(-T // MOBA_BLOCK)
    pad = nb * MOBA_BLOCK - T
    k = jnp.pad(k, ((0, 0), (0, pad), (0, 0), (0, 0)))
    v = jnp.pad(v, ((0, 0), (0, pad), (0, 0), (0, 0)))
    k_blocks = k.reshape(B, nb, MOBA_BLOCK, N_KV_HEADS, HEAD_DIM).transpose(0, 3, 1, 2, 4)
    v_blocks = v.reshape(B, nb, MOBA_BLOCK, N_KV_HEADS, HEAD_DIM).transpose(0, 3, 1, 2, 4)
    k_means = jnp.mean(k_blocks.astype(jnp.float32), axis=3).astype(k.dtype)
    return k_blocks, v_blocks, k_means


def memory_attention(qm, mem, w_mem_kv):
    B, T = qm.shape[0], qm.shape[1]
    M = mem.shape[1]
    mkv = mem @ w_mem_kv
    mk = mkv[..., :MQ_W].reshape(B, M, N_MEM_HEADS, HEAD_DIM)
    mv = mkv[..., MQ_W:].reshape(B, M, N_MEM_HEADS, HEAD_DIM)
    qh = qm.reshape(B, T, N_MEM_HEADS, HEAD_DIM)
    lg = jnp.einsum('bqhd,bmhd->bhqm', qh, mk).astype(jnp.float32) * (HEAD_DIM ** -0.5)
    p = jax.nn.softmax(lg, axis=-1)
    return jnp.einsum('bhqm,bmhd->bqhd', p.astype(mv.dtype), mv).reshape(B, T, MQ_W)


def setup_inputs(seed: int = 0) -> dict:
    key = jax.random.key(seed)
    ks = jax.random.split(key, 12)
    nrm = jax.random.normal
    s = D_MODEL ** -0.5
    x = nrm(ks[0], (BATCH, SEQ, D_MODEL), jnp.float32)
    mem = nrm(ks[1], (BATCH, MEM_LEN, D_MODEL), jnp.float32)
    col_a = jnp.concatenate([jnp.ones((Q_W + KV_W,)), jnp.full((KV_W,), BETA),
                             jnp.ones((MQ_W,))]).astype(jnp.float32)
    w_in_a = nrm(ks[2], (N_A_LAYERS, D_MODEL, IN_A_W), jnp.float32) * s * col_a
    sinks_a = nrm(ks[3], (N_A_LAYERS, N_Q_HEADS), jnp.float32) * 0.5
    w_q_b = nrm(ks[4], (N_B_LAYERS, D_MODEL, MIX_W), jnp.float32) * s
    col_kv = jnp.concatenate([jnp.ones((KV_W,)), jnp.full((KV_W,), BETA)]).astype(jnp.float32)
    w_kv_shared = nrm(ks[5], (D_MODEL, 2 * KV_W), jnp.float32) * s * col_kv
    col_m = jnp.concatenate([jnp.ones((MQ_W,)), jnp.full((MQ_W,), BETA)]).astype(jnp.float32)
    w_mem_kv = nrm(ks[6], (DEPTH, D_MODEL, 2 * MQ_W), jnp.float32) * s * col_m
    w_o = nrm(ks[7], (DEPTH, MIX_W, D_MODEL), jnp.float32) * (MIX_W ** -0.5) * BETA
    w_up = nrm(ks[8], (DEPTH, D_MODEL, D_FF), jnp.float32) * s
    w_down = nrm(ks[9], (DEPTH, D_FF, D_MODEL), jnp.float32) * (D_FF ** -0.5) * BETA
    ln_g = 1.0 + 0.02 * nrm(ks[10], (DEPTH, 2, D_MODEL), jnp.float32)
    ln_b = 0.02 * nrm(ks[11], (DEPTH, 2, D_MODEL), jnp.float32)
    return {"x": x, "mem": mem, "w_in_a": w_in_a, "sinks_a": sinks_a, "w_q_b": w_q_b,
            "w_kv_shared": w_kv_shared, "w_mem_kv": w_mem_kv, "w_o": w_o,
            "w_up": w_up, "w_down": w_down, "ln_g": ln_g, "ln_b": ln_b}


def reference(x, mem, w_in_a, sinks_a, w_q_b, w_kv_shared, w_mem_kv, w_o,
              w_up, w_down, ln_g, ln_b):
    B, T = x.shape[0], x.shape[1]
    slopes = alibi_slopes()
    h = x
    shared = None
    for layer in range(DEPTH):
        if layer < N_A_LAYERS:
            proj = h @ w_in_a[layer]
            q = proj[..., :Q_W].reshape(B, T, N_KV_HEADS, GROUP, HEAD_DIM)
            k = proj[..., Q_W:Q_W + KV_W].reshape(B, T, N_KV_HEADS, HEAD_DIM)
            v = proj[..., Q_W + KV_W:Q_W + 2 * KV_W].reshape(B, T, N_KV_HEADS, HEAD_DIM)
            qm = proj[..., Q_W + 2 * KV_W:]
            self_out = sliding_window_attention(q, k, v, sinks_a[layer], slopes)
        else:
            if layer == N_A_LAYERS:
                shared = shared_moba_kv(h, w_kv_shared)
            proj = h @ w_q_b[layer - N_A_LAYERS]
            q = proj[..., :Q_W].reshape(B, T, N_KV_HEADS, GROUP, HEAD_DIM)
            qm = proj[..., Q_W:]
            self_out = moba_attention(q, shared[0], shared[1], shared[2], slopes)
        mem_out = memory_attention(qm, mem, w_mem_kv[layer])
        mix = jnp.concatenate([self_out, mem_out], axis=-1) @ w_o[layer]
        h = layer_norm(ALPHA * h + mix, ln_g[layer, 0], ln_b[layer, 0])
        ffn = jnp.square(jax.nn.relu(h @ w_up[layer])) @ w_down[layer]
        h = layer_norm(ALPHA * h + ffn, ln_g[layer, 1], ln_b[layer, 1])
    return h
```

```python
import functools
import math

import jax
import jax.numpy as jnp
from jax import lax
from jax.experimental import pallas as pl
from jax.experimental.pallas import tpu as pltpu

D_MODEL = 2048
DEPTH = 4
HEAD_DIM = 128
N_Q_HEADS = 12
N_KV_HEADS = 4
GROUP = N_Q_HEADS // N_KV_HEADS
N_MEM_HEADS = 4
WINDOW = 128
MOBA_BLOCK = 256
MOBA_TOPK = 3
D_FF = 4 * D_MODEL
N_A_LAYERS = DEPTH // 2
ALPHA = (2 * DEPTH) ** 0.25
LN_EPS = 1e-5
NEG_INF = -1e30
Q_W = N_Q_HEADS * HEAD_DIM
KV_W = N_KV_HEADS * HEAD_DIM
MQ_W = N_MEM_HEADS * HEAD_DIM
MIX_W = Q_W + MQ_W
SCALE = HEAD_DIM ** -0.5
SLOPES = tuple(2.0 ** (-8.0 * h / N_Q_HEADS) for h in range(1, N_Q_HEADS + 1))

V7X_VMEM_BYTES = 64 * 1024 * 1024
VMEM_LIMIT = V7X_VMEM_BYTES - 8 * 1024 * 1024
LANES = 128

BF16 = jnp.bfloat16
F32 = jnp.float32


def _params(sem):
    return pltpu.CompilerParams(dimension_semantics=sem, vmem_limit_bytes=VMEM_LIMIT)


def _dot(a, b):
    return jnp.dot(a, b, preferred_element_type=F32)


def _dot_nt(a, b):
    return lax.dot_general(a, b, (((1,), (1,)), ((), ())), preferred_element_type=F32)


def _mm_kernel(x_ref, w_ref, o_ref):
    o_ref[...] = _dot(x_ref[...].astype(BF16), w_ref[...]).astype(o_ref.dtype)


def _matmul(x, w, tm, tn, name):
    m, k = x.shape
    n = w.shape[1]
    return pl.pallas_call(
        _mm_kernel,
        grid=(m // tm, n // tn),
        in_specs=[pl.BlockSpec((tm, k), lambda i, j: (i, 0)),
                  pl.BlockSpec((k, tn), lambda i, j: (0, j))],
        out_specs=pl.BlockSpec((tm, tn), lambda i, j: (i, j)),
        out_shape=jax.ShapeDtypeStruct((m, n), BF16),
        compiler_params=_params(("parallel", "arbitrary")),
        name=name,
    )(x, w)


def _kv_kernel(x_ref, w_ref, kv_ref, km_ref):
    acc = _dot(x_ref[...].astype(BF16), w_ref[...])
    kv_ref[...] = acc.astype(kv_ref.dtype)
    nblk = acc.shape[0] // MOBA_BLOCK
    k = acc[:, :KV_W].reshape(nblk, MOBA_BLOCK, KV_W)
    km_ref[...] = jnp.mean(k, axis=1)[:, None, :]


def _shared_kv(x, w, tm):
    m, k = x.shape
    n = w.shape[1]
    nblk = tm // MOBA_BLOCK
    return pl.pallas_call(
        _kv_kernel,
        grid=(m // tm,),
        in_specs=[pl.BlockSpec((tm, k), lambda i: (i, 0)),
                  pl.BlockSpec((k, n), lambda i: (0, 0))],
        out_specs=[pl.BlockSpec((tm, n), lambda i: (i, 0)),
                   pl.BlockSpec((nblk, 1, KV_W), lambda i: (i, 0, 0))],
        out_shape=[jax.ShapeDtypeStruct((m, n), BF16),
                   jax.ShapeDtypeStruct((m // MOBA_BLOCK, 1, KV_W), F32)],
        compiler_params=_params(("parallel",)),
        name="shared_kv",
    )(x, w)


def _layer_norm(r, g, b):
    mu = jnp.mean(r, axis=-1, keepdims=True)
    c = r - mu
    var = jnp.mean(c * c, axis=-1, keepdims=True)
    return c * lax.rsqrt(var + LN_EPS) * g + b


def _memory_heads(qm_ref, mk_ref, mv_ref, o_ref):
    for h in range(N_MEM_HEADS):
        cs = slice(h * HEAD_DIM, (h + 1) * HEAD_DIM)
        s = _dot_nt(qm_ref[0, :, cs], mk_ref[0, :, cs]) * SCALE
        m = jnp.max(s, axis=-1, keepdims=True)
        p = jnp.exp(s - m)
        l = jnp.sum(p, axis=-1, keepdims=True)
        o = _dot(p.astype(BF16), mv_ref[0, :, cs]) / l
        o_ref[0, :, Q_W + h * HEAD_DIM:Q_W + (h + 1) * HEAD_DIM] = o.astype(o_ref.dtype)


def _stack_group(q_ref, kvh):
    return jnp.concatenate(
        [q_ref[0, :, (kvh * GROUP + g) * HEAD_DIM:(kvh * GROUP + g + 1) * HEAD_DIM]
         for g in range(GROUP)], axis=0)


def _group_rows(tq, kvh, values):
    return jnp.concatenate(
        [jnp.full((tq, 1), values[kvh * GROUP + g], F32) for g in range(GROUP)], axis=0)


def _unstack_group(o_ref, kvh, out, tq):
    for g in range(GROUP):
        h = kvh * GROUP + g
        o_ref[0, :, h * HEAD_DIM:(h + 1) * HEAD_DIM] = out[g * tq:(g + 1) * tq].astype(o_ref.dtype)


def _swa_kernel(sink_ref, q_ref, kp_ref, kc_ref, vp_ref, vc_ref, qm_ref, mk_ref, mv_ref, o_ref):
    tq = q_ref.shape[1]
    n = pl.program_id(1)
    rows = GROUP * tq
    nk = WINDOW + tq
    r_io = lax.broadcasted_iota(jnp.int32, (tq, nk), 0)
    c_io = lax.broadcasted_iota(jnp.int32, (tq, nk), 1)
    dist1 = r_io + WINDOW - c_io
    first_col = jnp.where(n > 0, 0, WINDOW)
    valid1 = (dist1 >= 0) & (dist1 < WINDOW) & (c_io >= first_col)
    dist = jnp.concatenate([dist1] * GROUP, axis=0).astype(F32)
    valid = jnp.concatenate([valid1] * GROUP, axis=0)
    for kvh in range(N_KV_HEADS):
        cs = slice(kvh * HEAD_DIM, (kvh + 1) * HEAD_DIM)
        q = _stack_group(q_ref, kvh)
        kw = jnp.concatenate([kp_ref[0, :, cs], kc_ref[0, :, cs]], axis=0)
        vw = jnp.concatenate([vp_ref[0, :, cs], vc_ref[0, :, cs]], axis=0)
        slope = _group_rows(tq, kvh, SLOPES)
        sink = jnp.concatenate(
            [jnp.full((tq, 1), sink_ref[kvh * GROUP + g], F32) for g in range(GROUP)], axis=0)
        s = _dot_nt(q, kw) * SCALE - slope * dist
        s = jnp.where(valid, s, NEG_INF)
        m = jnp.maximum(jnp.max(s, axis=-1, keepdims=True), sink)
        p = jnp.exp(s - m)
        denom = jnp.sum(p, axis=-1, keepdims=True) + jnp.exp(sink - m)
        out = _dot(p.astype(BF16), vw) / denom
        _unstack_group(o_ref, kvh, out, tq)
    _memory_heads(qm_ref, mk_ref, mv_ref, o_ref)


def _swa_layer(proj, mkv, sinks, tq):
    b, t, _ = proj.shape
    mlen = mkv.shape[1]
    wpb = tq // WINDOW
    kcol = Q_W // KV_W
    prev = lambda bi, n: (bi, jnp.maximum(n * wpb - 1, 0))
    return pl.pallas_call(
        _swa_kernel,
        grid=(b, t // tq),
        in_specs=[
            pl.BlockSpec(memory_space=pltpu.SMEM),
            pl.BlockSpec((1, tq, Q_W), lambda bi, n: (bi, n, 0)),
            pl.BlockSpec((1, WINDOW, KV_W), lambda bi, n: prev(bi, n) + (kcol,)),
            pl.BlockSpec((1, tq, KV_W), lambda bi, n: (bi, n, kcol)),
            pl.BlockSpec((1, WINDOW, KV_W), lambda bi, n: prev(bi, n) + (kcol + 1,)),
            pl.BlockSpec((1, tq, KV_W), lambda bi, n: (bi, n, kcol + 1)),
            pl.BlockSpec((1, tq, MQ_W), lambda bi, n: (bi, n, kcol + 2)),
            pl.BlockSpec((1, mlen, MQ_W), lambda bi, n: (bi, 0, 0)),
            pl.BlockSpec((1, mlen, MQ_W), lambda bi, n: (bi, 0, 1)),
        ],
        out_specs=pl.BlockSpec((1, tq, MIX_W), lambda bi, n: (bi, n, 0)),
        out_shape=jax.ShapeDtypeStruct((b, t, MIX_W), BF16),
        compiler_params=_params(("parallel", "arbitrary")),
        name="swa_mem_attn",
    )(sinks, proj, proj, proj, proj, proj, proj, mkv, mkv)


def _moba_kernel(q_ref, k_ref, v_ref, km_ref, qm_ref, mk_ref, mv_ref, o_ref, m_ref, l_ref, acc_ref):
    tq = MOBA_BLOCK
    j = pl.program_id(1)
    rows = GROUP * tq
    r_io = lax.broadcasted_iota(jnp.int32, (tq, tq), 0)
    c_io = lax.broadcasted_iota(jnp.int32, (tq, tq), 1)
    rc1 = r_io - c_io
    rc = jnp.concatenate([rc1] * GROUP, axis=0).astype(F32)
    causal = jnp.concatenate([rc1 >= 0] * GROUP, axis=0)
    lane = lax.broadcasted_iota(jnp.int32, (rows, LANES), 1)
    for kvh in range(N_KV_HEADS):
        cs = slice(kvh * HEAD_DIM, (kvh + 1) * HEAD_DIM)
        q = _stack_group(q_ref, kvh)
        slope = _group_rows(tq, kvh, SLOPES)
        gate = _dot_nt(q, km_ref[0, :, cs].astype(BF16))
        gate = jnp.where(lane < j, gate, NEG_INF)
        cnt = jnp.zeros((rows, LANES), F32)
        for ip in range(k_ref.shape[1] // tq):
            gi = gate[:, ip:ip + 1]
            beats = (gi > gate) | ((gi == gate) & (ip < lane))
            cnt = cnt + jnp.where(beats, 1.0, 0.0)
        sel = jnp.where((cnt < MOBA_TOPK) & (lane < j), 1.0, 0.0)

        start = pl.multiple_of(j * tq, tq)
        kb = k_ref[0, pl.ds(start, tq), cs]
        vb = v_ref[0, pl.ds(start, tq), cs]
        s = _dot_nt(q, kb) * SCALE - slope * rc
        s = jnp.where(causal, s, NEG_INF)
        m0 = jnp.max(s, axis=-1, keepdims=True)
        p = jnp.exp(s - m0)
        m_ref[...] = m0
        l_ref[...] = jnp.sum(p, axis=-1, keepdims=True)
        acc_ref[...] = _dot(p.astype(BF16), vb)

        def past_block(i, carry):
            off = pl.multiple_of(i * tq, tq)
            kb = k_ref[0, pl.ds(off, tq), cs]
            vb = v_ref[0, pl.ds(off, tq), cs]
            gap = ((j - i) * tq).astype(F32)
            s = _dot_nt(q, kb) * SCALE - slope * (rc + gap)
            chosen = jnp.sum(jnp.where(lane == i, sel, 0.0), axis=-1, keepdims=True)
            s = jnp.where(chosen > 0.5, s, NEG_INF)
            m_old = m_ref[...]
            m_new = jnp.maximum(m_old, jnp.max(s, axis=-1, keepdims=True))
            a = jnp.exp(m_old - m_new)
            p = jnp.exp(s - m_new)
            l_ref[...] = a * l_ref[...] + jnp.sum(p, axis=-1, keepdims=True)
            acc_ref[...] = a * acc_ref[...] + _dot(p.astype(BF16), vb)
            m_ref[...] = m_new
            return carry

        lax.fori_loop(0, j, past_block, 0)
        _unstack_group(o_ref, kvh, acc_ref[...] / l_ref[...], tq)
    _memory_heads(qm_ref, mk_ref, mv_ref, o_ref)


def _moba_layer(proj, kv, kmeans, mkv):
    b, t, _ = proj.shape
    mlen = mkv.shape[1]
    tq = MOBA_BLOCK
    rows = GROUP * tq
    return pl.pallas_call(
        _moba_kernel,
        grid=(b, t // tq),
        in_specs=[
            pl.BlockSpec((1, tq, Q_W), lambda bi, n: (bi, n, 0)),
            pl.BlockSpec((1, t, KV_W), lambda bi, n: (bi, 0, 0)),
            pl.BlockSpec((1, t, KV_W), lambda bi, n: (bi, 0, 1)),
            pl.BlockSpec((1, LANES, KV_W), lambda bi, n: (bi, 0, 0)),
            pl.BlockSpec((1, tq, MQ_W), lambda bi, n: (bi, n, Q_W // MQ_W)),
            pl.BlockSpec((1, mlen, MQ_W), lambda bi, n: (bi, 0, 0)),
            pl.BlockSpec((1, mlen, MQ_W), lambda bi, n: (bi, 0, 1)),
        ],
        out_specs=pl.BlockSpec((1, tq, MIX_W), lambda bi, n: (bi, n, 0)),
        out_shape=jax.ShapeDtypeStruct((b, t, MIX_W), BF16),
        scratch_shapes=[pltpu.VMEM((rows, 1), F32), pltpu.VMEM((rows, 1), F32),
                        pltpu.VMEM((rows, HEAD_DIM), F32)],
        compiler_params=_params(("parallel", "arbitrary")),
        name="moba_mem_attn",
    )(proj, kv, kv, kmeans, proj, mkv, mkv)


def _out_proj_kernel(mix_ref, w_ref, h_ref, g_ref, b_ref, o_ref):
    r = ALPHA * h_ref[...] + _dot(mix_ref[...], w_ref[...])
    o_ref[...] = _layer_norm(r, g_ref[...], b_ref[...])


def _out_proj_ln(mix, w, h, g, bias, tm):
    m, k = mix.shape
    n = w.shape[1]
    return pl.pallas_call(
        _out_proj_kernel,
        grid=(m // tm,),
        in_specs=[pl.BlockSpec((tm, k), lambda i: (i, 0)),
                  pl.BlockSpec((k, n), lambda i: (0, 0)),
                  pl.BlockSpec((tm, n), lambda i: (i, 0)),
                  pl.BlockSpec((1, n), lambda i: (0, 0)),
                  pl.BlockSpec((1, n), lambda i: (0, 0))],
        out_specs=pl.BlockSpec((tm, n), lambda i: (i, 0)),
        out_shape=jax.ShapeDtypeStruct((m, n), F32),
        compiler_params=_params(("parallel",)),
        name="out_proj_ln",
    )(mix, w, h, g, bias)


def _ffn_kernel(h_ref, wu_ref, wd_ref, g_ref, b_ref, o_ref, hb_ref, acc_ref):
    f = pl.program_id(1)

    @pl.when(f == 0)
    def _():
        hb_ref[...] = h_ref[...].astype(BF16)
        acc_ref[...] = jnp.zeros_like(acc_ref)

    u = jnp.maximum(_dot(hb_ref[...], wu_ref[...]), 0.0)
    acc_ref[...] += _dot((u * u).astype(BF16), wd_ref[...])

    @pl.when(f == pl.num_programs(1) - 1)
    def _():
        r = ALPHA * h_ref[...] + acc_ref[...]
        o_ref[...] = _layer_norm(r, g_ref[...], b_ref[...])


def _ffn_ln(h, w_up, w_down, g, bias, tm, tf):
    m, d = h.shape
    dff = w_up.shape[1]
    return pl.pallas_call(
        _ffn_kernel,
        grid=(m // tm, dff // tf),
        in_specs=[pl.BlockSpec((tm, d), lambda i, f: (i, 0)),
                  pl.BlockSpec((d, tf), lambda i, f: (0, f)),
                  pl.BlockSpec((tf, d), lambda i, f: (f, 0)),
                  pl.BlockSpec((1, d), lambda i, f: (0, 0)),
                  pl.BlockSpec((1, d), lambda i, f: (0, 0))],
        out_specs=pl.BlockSpec((tm, d), lambda i, f: (i, 0)),
        out_shape=jax.ShapeDtypeStruct((m, d), F32),
        scratch_shapes=[pltpu.VMEM((tm, d), BF16), pltpu.VMEM((tm, d), F32)],
        compiler_params=_params(("parallel", "arbitrary")),
        name="ffn_ln",
    )(h, w_up, w_down, g, bias)


PROJ_TM, PROJ_TN = 1024, 512
KV_TM = 512
OUT_TM = 512
FFN_TM, FFN_TF = 512, 512
SWA_TQ = 256


def kernel(x, mem, w_in_a, sinks_a, w_q_b, w_kv_shared, w_mem_kv, w_o, w_up, w_down, ln_g, ln_b):
    b, t, d = x.shape
    mlen = mem.shape[1]
    m = b * t
    nb = t // MOBA_BLOCK
    h = x.reshape(m, d)
    mem2 = mem.reshape(b * mlen, d)
    kv = kmeans = None
    for layer in range(DEPTH):
        mkv = _matmul(mem2, w_mem_kv[layer].astype(BF16), b * mlen, 512, "mem_kv")
        mkv = mkv.reshape(b, mlen, 2 * MQ_W)
        if layer < N_A_LAYERS:
            proj = _matmul(h, w_in_a[layer].astype(BF16), PROJ_TM, PROJ_TN, "proj_a")
            mix = _swa_layer(proj.reshape(b, t, -1), mkv, sinks_a[layer], SWA_TQ)
        else:
            if kv is None:
                kv, km = _shared_kv(h, w_kv_shared.astype(BF16), KV_TM)
                kv = kv.reshape(b, t, 2 * KV_W)
                kmeans = jnp.pad(km.reshape(b, nb, KV_W), ((0, 0), (0, LANES - nb), (0, 0)))
            proj = _matmul(h, w_q_b[layer - N_A_LAYERS].astype(BF16), PROJ_TM, PROJ_TN, "proj_b")
            mix = _moba_layer(proj.reshape(b, t, -1), kv, kmeans, mkv)
        h = _out_proj_ln(mix.reshape(m, MIX_W), w_o[layer].astype(BF16), h,
                         ln_g[layer, 0][None], ln_b[layer, 0][None], OUT_TM)
        h = _ffn_ln(h, w_up[layer].astype(BF16), w_down[layer].astype(BF16),
                    ln_g[layer, 1][None], ln_b[layer, 1][None], FFN_TM, FFN_TF)
    return h.reshape(b, t, d)
```

```python
import jax
import jax.numpy as jnp
from jax import lax
from jax.experimental import pallas as pl
from jax.experimental.pallas import tpu as pltpu

D_MODEL = 2048
DEPTH = 4
HEAD_DIM = 128
N_Q_HEADS = 12
N_KV_HEADS = 4
GROUP = N_Q_HEADS // N_KV_HEADS
N_MEM_HEADS = 4
WINDOW = 128
MOBA_BLOCK = 256
MOBA_TOPK = 3
D_FF = 4 * D_MODEL
N_A_LAYERS = DEPTH // 2
ALPHA = (2 * DEPTH) ** 0.25
LN_EPS = 1e-5
NEG_INF = -1e30
Q_W = N_Q_HEADS * HEAD_DIM
KV_W = N_KV_HEADS * HEAD_DIM
MQ_W = N_MEM_HEADS * HEAD_DIM
MIX_W = Q_W + MQ_W
SCALE = HEAD_DIM ** -0.5
SLOPES = tuple(2.0 ** (-8.0 * h / N_Q_HEADS) for h in range(1, N_Q_HEADS + 1))

V7X_VMEM_BYTES = 64 * 1024 * 1024
VMEM_LIMIT = V7X_VMEM_BYTES - 8 * 1024 * 1024

BF16 = jnp.bfloat16
F32 = jnp.float32


def _params(sem):
    return pltpu.CompilerParams(dimension_semantics=sem, vmem_limit_bytes=VMEM_LIMIT)


def _dot(a, b):
    return jnp.dot(a, b, preferred_element_type=F32)


def _dot_nt(a, b):
    return lax.dot_general(a, b, (((1,), (1,)), ((), ())), preferred_element_type=F32)


def _head(h):
    return slice(h * HEAD_DIM, (h + 1) * HEAD_DIM)


def _mm_kernel(x_ref, w_ref, o_ref):
    o_ref[...] = _dot(x_ref[...].astype(BF16), w_ref[...]).astype(o_ref.dtype)


def _matmul(x, w, tm, tn, name):
    m, k = x.shape
    n = w.shape[1]
    return pl.pallas_call(
        _mm_kernel,
        grid=(m // tm, n // tn),
        in_specs=[pl.BlockSpec((tm, k), lambda i, j: (i, 0)),
                  pl.BlockSpec((k, tn), lambda i, j: (0, j))],
        out_specs=pl.BlockSpec((tm, tn), lambda i, j: (i, j)),
        out_shape=jax.ShapeDtypeStruct((m, n), BF16),
        compiler_params=_params(("parallel", "arbitrary")),
        name=name,
    )(x, w)


def _kv_kernel(x_ref, w_ref, kv_ref, km_ref):
    acc = _dot(x_ref[...].astype(BF16), w_ref[...])
    kv_ref[...] = acc.astype(kv_ref.dtype)
    nblk = acc.shape[0] // MOBA_BLOCK
    k = acc[:, :KV_W].reshape(nblk, MOBA_BLOCK, KV_W)
    km_ref[...] = jnp.mean(k, axis=1)[:, None, :]


def _shared_kv(x, w, tm):
    m, k = x.shape
    n = w.shape[1]
    nblk = tm // MOBA_BLOCK
    return pl.pallas_call(
        _kv_kernel,
        grid=(m // tm,),
        in_specs=[pl.BlockSpec((tm, k), lambda i: (i, 0)),
                  pl.BlockSpec((k, n), lambda i: (0, 0))],
        out_specs=[pl.BlockSpec((tm, n), lambda i: (i, 0)),
                   pl.BlockSpec((nblk, 1, KV_W), lambda i: (i, 0, 0))],
        out_shape=[jax.ShapeDtypeStruct((m, n), BF16),
                   jax.ShapeDtypeStruct((m // MOBA_BLOCK, 1, KV_W), F32)],
        compiler_params=_params(("parallel",)),
        name="shared_kv",
    )(x, w)


def _layer_norm(r, g, b):
    mu = jnp.mean(r, axis=-1, keepdims=True)
    c = r - mu
    var = jnp.mean(c * c, axis=-1, keepdims=True)
    return c * lax.rsqrt(var + LN_EPS) * g + b


def _memory_heads(qm_ref, mk_ref, mvt_ref, o_ref):
    for h in range(N_MEM_HEADS):
        s = _dot_nt(mk_ref[0, :, _head(h)], qm_ref[0, :, _head(h)]) * SCALE
        m = jnp.max(s, axis=0, keepdims=True)
        p = jnp.exp(s - m)
        l = jnp.sum(p, axis=0, keepdims=True)
        ot = _dot(mvt_ref[0, _head(h), :], p.astype(BF16)) / l
        o_ref[0, :, Q_W + h * HEAD_DIM:Q_W + (h + 1) * HEAD_DIM] = ot.T.astype(o_ref.dtype)


def _swa_kernel(sink_ref, q_ref, kp_ref, kc_ref, vtp_ref, vtc_ref, qm_ref, mk_ref, mvt_ref, o_ref):
    tq = q_ref.shape[1]
    n = pl.program_id(1)
    sw = WINDOW
    c_io = lax.broadcasted_iota(jnp.int32, (2 * sw, sw), 0)
    r_io = lax.broadcasted_iota(jnp.int32, (2 * sw, sw), 1)
    dist_i = r_io + sw - c_io
    band = (dist_i >= 0) & (dist_i < sw)
    dist = dist_i.astype(F32)
    first_row = jnp.where(n > 0, 0, sw)
    band_first = band & (c_io >= first_row)
    for sb in range(tq // sw):
        rows = slice(sb * sw, (sb + 1) * sw)
        valid = band_first if sb == 0 else band
        for kvh in range(N_KV_HEADS):
            cs = _head(kvh)
            if sb == 0:
                kw = jnp.concatenate([kp_ref[0, :, cs], kc_ref[0, :sw, cs]], axis=0)
                vtw = jnp.concatenate([vtp_ref[0, cs, :], vtc_ref[0, cs, :sw]], axis=1)
            else:
                kw = kc_ref[0, (sb - 1) * sw:(sb + 1) * sw, cs]
                vtw = vtc_ref[0, cs, (sb - 1) * sw:(sb + 1) * sw]
            heads = [kvh * GROUP + g for g in range(GROUP)]
            q = jnp.concatenate([q_ref[0, rows, _head(h)] for h in heads], axis=0)
            s_all = _dot_nt(kw, q) * SCALE
            ps, denoms = [], []
            for g, h in enumerate(heads):
                s = s_all[:, g * sw:(g + 1) * sw] - SLOPES[h] * dist
                s = jnp.where(valid, s, NEG_INF)
                sink = sink_ref[h]
                m = jnp.maximum(jnp.max(s, axis=0, keepdims=True), sink)
                p = jnp.exp(s - m)
                denoms.append(jnp.sum(p, axis=0, keepdims=True) + jnp.exp(sink - m))
                ps.append(p.astype(BF16))
            ot_all = _dot(vtw, jnp.concatenate(ps, axis=1))
            for g, h in enumerate(heads):
                ot = ot_all[:, g * sw:(g + 1) * sw] / denoms[g]
                o_ref[0, rows, _head(h)] = ot.T.astype(o_ref.dtype)
    _memory_heads(qm_ref, mk_ref, mvt_ref, o_ref)


def _swa_layer(proj, vt, mkv, mvt, sinks, tq):
    b, t, _ = proj.shape
    mlen = mkv.shape[1]
    wpb = tq // WINDOW
    kcol = Q_W // KV_W
    prev = lambda n: jnp.maximum(n * wpb - 1, 0)
    return pl.pallas_call(
        _swa_kernel,
        grid=(b, t // tq),
        in_specs=[
            pl.BlockSpec(memory_space=pltpu.SMEM),
            pl.BlockSpec((1, tq, Q_W), lambda bi, n: (bi, n, 0)),
            pl.BlockSpec((1, WINDOW, KV_W), lambda bi, n: (bi, prev(n), kcol)),
            pl.BlockSpec((1, tq, KV_W), lambda bi, n: (bi, n, kcol)),
            pl.BlockSpec((1, KV_W, WINDOW), lambda bi, n: (bi, 0, prev(n))),
            pl.BlockSpec((1, KV_W, tq), lambda bi, n: (bi, 0, n)),
            pl.BlockSpec((1, tq, MQ_W), lambda bi, n: (bi, n, kcol + 2)),
            pl.BlockSpec((1, mlen, MQ_W), lambda bi, n: (bi, 0, 0)),
            pl.BlockSpec((1, MQ_W, mlen), lambda bi, n: (bi, 0, 0)),
        ],
        out_specs=pl.BlockSpec((1, tq, MIX_W), lambda bi, n: (bi, n, 0)),
        out_shape=jax.ShapeDtypeStruct((b, t, MIX_W), BF16),
        compiler_params=_params(("parallel", "arbitrary")),
        name="swa_mem_attn",
    )(sinks, proj, proj, proj, vt, vt, proj, mkv, mvt)


def _moba_kernel(q_ref, k_ref, vt_ref, km_ref, qm_ref, mk_ref, mvt_ref, o_ref,
                 al_ref, sel_ref, m_ref, l_ref, acc_ref):
    tq = MOBA_BLOCK
    nb = k_ref.shape[1]
    j = pl.program_id(1)
    key = lax.broadcasted_iota(jnp.int32, (tq, tq), 0)
    qry = lax.broadcasted_iota(jnp.int32, (tq, tq), 1)
    causal = key <= qry
    s_minus_t = (key - qry).astype(F32)
    blk = lax.broadcasted_iota(jnp.int32, (nb, tq), 0)
    for kvh in range(N_KV_HEADS):
        cs = _head(kvh)
        heads = [kvh * GROUP + g for g in range(GROUP)]
        kmb = km_ref[0, :, cs].astype(BF16)
        k_own = k_ref[0, j, :, cs]
        vt_own = vt_ref[0, j, cs, :]
        for g, h in enumerate(heads):
            qh = q_ref[0, :, _head(h)]
            gate = jnp.where(blk < j, _dot_nt(kmb, qh), NEG_INF)
            cnt = jnp.zeros((nb, tq), F32)
            for ip in range(nb):
                gi = gate[ip:ip + 1, :]
                beats = (gi > gate) | ((gi == gate) & (ip < blk))
                cnt = cnt + jnp.where(beats, 1.0, 0.0)
            sel = jnp.where((cnt < MOBA_TOPK) & (blk < j), 0.0, NEG_INF)
            for ip in range(nb):
                sel_ref[g, ip] = sel[ip:ip + 1, :]
            al = SLOPES[h] * s_minus_t
            al_ref[g] = al
            s = jnp.where(causal, _dot_nt(k_own, qh) * SCALE + al, NEG_INF)
            m0 = jnp.max(s, axis=0, keepdims=True)
            p = jnp.exp(s - m0)
            m_ref[g] = m0
            l_ref[g] = jnp.sum(p, axis=0, keepdims=True)
            acc_ref[g] = _dot(vt_own, p.astype(BF16))

        def past_block(i, carry):
            kb = k_ref[0, i, :, cs]
            vtb = vt_ref[0, i, cs, :]
            gap = ((j - i) * tq).astype(F32)
            for g, h in enumerate(heads):
                qh = q_ref[0, :, _head(h)]
                rowc = sel_ref[g, i] - SLOPES[h] * gap
                s = _dot_nt(kb, qh) * SCALE + al_ref[g] + rowc
                m_old = m_ref[g]
                m_new = jnp.maximum(m_old, jnp.max(s, axis=0, keepdims=True))
                a = jnp.exp(m_old - m_new)
                p = jnp.exp(s - m_new)
                l_ref[g] = a * l_ref[g] + jnp.sum(p, axis=0, keepdims=True)
                acc_ref[g] = a * acc_ref[g] + _dot(vtb, p.astype(BF16))
                m_ref[g] = m_new
            return carry

        lax.fori_loop(0, j, past_block, 0)
        for g, h in enumerate(heads):
            ot = acc_ref[g] / l_ref[g]
            o_ref[0, :, _head(h)] = ot.T.astype(o_ref.dtype)
    _memory_heads(qm_ref, mk_ref, mvt_ref, o_ref)


def _moba_layer(proj, k4, vt4, kmeans, mkv, mvt):
    b, t, _ = proj.shape
    mlen = mkv.shape[1]
    tq = MOBA_BLOCK
    nb = t // tq
    return pl.pallas_call(
        _moba_kernel,
        grid=(b, nb),
        in_specs=[
            pl.BlockSpec((1, tq, Q_W), lambda bi, n: (bi, n, 0)),
            pl.BlockSpec((1, nb, tq, KV_W), lambda bi, n: (bi, 0, 0, 0)),
            pl.BlockSpec((1, nb, KV_W, tq), lambda bi, n: (bi, 0, 0, 0)),
            pl.BlockSpec((1, nb, KV_W), lambda bi, n: (bi, 0, 0)),
            pl.BlockSpec((1, tq, MQ_W), lambda bi, n: (bi, n, Q_W // MQ_W)),
            pl.BlockSpec((1, mlen, MQ_W), lambda bi, n: (bi, 0, 0)),
            pl.BlockSpec((1, MQ_W, mlen), lambda bi, n: (bi, 0, 0)),
        ],
        out_specs=pl.BlockSpec((1, tq, MIX_W), lambda bi, n: (bi, n, 0)),
        out_shape=jax.ShapeDtypeStruct((b, t, MIX_W), BF16),
        scratch_shapes=[pltpu.VMEM((GROUP, tq, tq), F32),
                        pltpu.VMEM((GROUP, nb, 1, tq), F32),
                        pltpu.VMEM((GROUP, 1, tq), F32),
                        pltpu.VMEM((GROUP, 1, tq), F32),
                        pltpu.VMEM((GROUP, HEAD_DIM, tq), F32)],
        compiler_params=_params(("parallel", "arbitrary")),
        name="moba_mem_attn",
    )(proj, k4, vt4, kmeans, proj, mkv, mvt)


def _out_proj_kernel(mix_ref, w_ref, h_ref, g_ref, b_ref, o_ref):
    r = ALPHA * h_ref[...] + _dot(mix_ref[...], w_ref[...])
    o_ref[...] = _layer_norm(r, g_ref[...], b_ref[...])


def _out_proj_ln(mix, w, h, g, bias, tm):
    m, k = mix.shape
    n = w.shape[1]
    return pl.pallas_call(
        _out_proj_kernel,
        grid=(m // tm,),
        in_specs=[pl.BlockSpec((tm, k), lambda i: (i, 0)),
                  pl.BlockSpec((k, n), lambda i: (0, 0)),
                  pl.BlockSpec((tm, n), lambda i: (i, 0)),
                  pl.BlockSpec((1, n), lambda i: (0, 0)),
                  pl.BlockSpec((1, n), lambda i: (0, 0))],
        out_specs=pl.BlockSpec((tm, n), lambda i: (i, 0)),
        out_shape=jax.ShapeDtypeStruct((m, n), F32),
        compiler_params=_params(("parallel",)),
        name="out_proj_ln",
    )(mix, w, h, g, bias)


def _ffn_kernel(h_ref, wu_ref, wd_ref, g_ref, b_ref, o_ref, hb_ref, acc_ref):
    f = pl.program_id(1)

    @pl.when(f == 0)
    def _():
        hb_ref[...] = h_ref[...].astype(BF16)
        acc_ref[...] = jnp.zeros_like(acc_ref)

    u = jnp.maximum(_dot(hb_ref[...], wu_ref[...]), 0.0)
    acc_ref[...] += _dot((u * u).astype(BF16), wd_ref[...])

    @pl.when(f == pl.num_programs(1) - 1)
    def _():
        r = ALPHA * h_ref[...] + acc_ref[...]
        o_ref[...] = _layer_norm(r, g_ref[...], b_ref[...])


def _ffn_ln(h, w_up, w_down, g, bias, tm, tf):
    m, d = h.shape
    dff = w_up.shape[1]
    return pl.pallas_call(
        _ffn_kernel,
        grid=(m // tm, dff // tf),
        in_specs=[pl.BlockSpec((tm, d), lambda i, f: (i, 0)),
                  pl.BlockSpec((d, tf), lambda i, f: (0, f)),
                  pl.BlockSpec((tf, d), lambda i, f: (f, 0)),
                  pl.BlockSpec((1, d), lambda i, f: (0, 0)),
                  pl.BlockSpec((1, d), lambda i, f: (0, 0))],
        out_specs=pl.BlockSpec((tm, d), lambda i, f: (i, 0)),
        out_shape=jax.ShapeDtypeStruct((m, d), F32),
        scratch_shapes=[pltpu.VMEM((tm, d), BF16), pltpu.VMEM((tm, d), F32)],
        compiler_params=_params(("parallel", "arbitrary")),
        name="ffn_ln",
    )(h, w_up, w_down, g, bias)


PROJ_TM, PROJ_TN = 1024, 512
KV_TM = 512
OUT_TM = 512
FFN_TM, FFN_TF = 512, 512
SWA_TQ = 512


def kernel(x, mem, w_in_a, sinks_a, w_q_b, w_kv_shared, w_mem_kv, w_o, w_up, w_down, ln_g, ln_b):
    b, t, d = x.shape
    mlen = mem.shape[1]
    m = b * t
    nb = t // MOBA_BLOCK
    h = x.reshape(m, d)
    mem2 = mem.reshape(b * mlen, d)
    k4 = vt4 = kmeans = None
    for layer in range(DEPTH):
        mkv = _matmul(mem2, w_mem_kv[layer].astype(BF16), b * mlen, 512, "mem_kv")
        mkv = mkv.reshape(b, mlen, 2 * MQ_W)
        mvt = jnp.swapaxes(mkv[:, :, MQ_W:], 1, 2)
        if layer < N_A_LAYERS:
            proj = _matmul(h, w_in_a[layer].astype(BF16), PROJ_TM, PROJ_TN, "proj_a")
            proj = proj.reshape(b, t, -1)
            vt = jnp.swapaxes(proj[:, :, Q_W + KV_W:Q_W + 2 * KV_W], 1, 2)
            mix = _swa_layer(proj, vt, mkv, mvt, sinks_a[layer], SWA_TQ)
        else:
            if k4 is None:
                kv, km = _shared_kv(h, w_kv_shared.astype(BF16), KV_TM)
                k4 = kv.reshape(b, nb, MOBA_BLOCK, 2 * KV_W)
                vt4 = jnp.swapaxes(k4[..., KV_W:], 2, 3)
                kmeans = km.reshape(b, nb, KV_W)
            proj = _matmul(h, w_q_b[layer - N_A_LAYERS].astype(BF16), PROJ_TM, PROJ_TN, "proj_b")
            mix = _moba_layer(proj.reshape(b, t, -1), k4, vt4, kmeans, mkv, mvt)
        h = _out_proj_ln(mix.reshape(m, MIX_W), w_o[layer].astype(BF16), h,
                         ln_g[layer, 0][None], ln_b[layer, 0][None], OUT_TM)
        h = _ffn_ln(h, w_up[layer].astype(BF16), w_down[layer].astype(BF16),
                    ln_g[layer, 1][None], ln_b[layer, 1][None], FFN_TM, FFN_TF)
    return h.reshape(b, t, d)
```

```python
import jax
import jax.numpy as jnp
from jax import lax
from jax.experimental import pallas as pl
from jax.experimental.pallas import tpu as pltpu

D_MODEL = 2048
DEPTH = 4
HEAD_DIM = 128
N_Q_HEADS = 12
N_KV_HEADS = 4
GROUP = N_Q_HEADS // N_KV_HEADS
N_MEM_HEADS = 4
WINDOW = 128
MOBA_BLOCK = 256
MOBA_TOPK = 3
D_FF = 4 * D_MODEL
N_A_LAYERS = DEPTH // 2
ALPHA = (2 * DEPTH) ** 0.25
LN_EPS = 1e-5
NEG_INF = -1e30
Q_W = N_Q_HEADS * HEAD_DIM
KV_W = N_KV_HEADS * HEAD_DIM
MQ_W = N_MEM_HEADS * HEAD_DIM
MIX_W = Q_W + MQ_W
SCALE = HEAD_DIM ** -0.5
SLOPES = tuple(2.0 ** (-8.0 * h / N_Q_HEADS) for h in range(1, N_Q_HEADS + 1))

V7X_VMEM_BYTES = 64 * 1024 * 1024
VMEM_LIMIT = V7X_VMEM_BYTES - 8 * 1024 * 1024
LANES = 128

BF16 = jnp.bfloat16
F32 = jnp.float32


def _params(sem):
    return pltpu.CompilerParams(dimension_semantics=sem, vmem_limit_bytes=VMEM_LIMIT)


def _dot(a, b):
    return jnp.dot(a, b, preferred_element_type=F32)


def _dot_nt(a, b):
    return lax.dot_general(a, b, (((1,), (1,)), ((), ())), preferred_element_type=F32)


def _head(h):
    return slice(h * HEAD_DIM, (h + 1) * HEAD_DIM)


def _lane_chunk(c):
    return slice(c * LANES, (c + 1) * LANES)


def _mm_kernel(x_ref, w_ref, o_ref, xb_ref):
    @pl.when(pl.program_id(1) == 0)
    def _():
        xb_ref[...] = x_ref[...].astype(BF16)

    o_ref[...] = _dot(xb_ref[...], w_ref[...].astype(BF16)).astype(o_ref.dtype)


def _matmul(x, w, layer, tm, tn, name):
    m, k = x.shape
    n = w.shape[2]
    return pl.pallas_call(
        _mm_kernel,
        grid=(m // tm, n // tn),
        in_specs=[pl.BlockSpec((tm, k), lambda i, j: (i, 0)),
                  pl.BlockSpec((None, k, tn), lambda i, j: (layer, 0, j))],
        out_specs=pl.BlockSpec((tm, tn), lambda i, j: (i, j)),
        out_shape=jax.ShapeDtypeStruct((m, n), BF16),
        scratch_shapes=[pltpu.VMEM((tm, k), BF16)],
        compiler_params=_params(("parallel", "arbitrary")),
        name=name,
    )(x, w)


def _kv_kernel(x_ref, w_ref, kv_ref, km_ref):
    acc = _dot(x_ref[...].astype(BF16), w_ref[...].astype(BF16))
    kv_ref[...] = acc.astype(kv_ref.dtype)
    nblk = acc.shape[0] // MOBA_BLOCK
    k = acc[:, :KV_W].reshape(nblk, MOBA_BLOCK, KV_W)
    km_ref[...] = jnp.mean(k, axis=1)[:, None, :]


def _shared_kv(x, w, tm):
    m, k = x.shape
    n = w.shape[1]
    nblk = tm // MOBA_BLOCK
    return pl.pallas_call(
        _kv_kernel,
        grid=(m // tm,),
        in_specs=[pl.BlockSpec((tm, k), lambda i: (i, 0)),
                  pl.BlockSpec((k, n), lambda i: (0, 0))],
        out_specs=[pl.BlockSpec((tm, n), lambda i: (i, 0)),
                   pl.BlockSpec((nblk, 1, KV_W), lambda i: (i, 0, 0))],
        out_shape=[jax.ShapeDtypeStruct((m, n), BF16),
                   jax.ShapeDtypeStruct((m // MOBA_BLOCK, 1, KV_W), F32)],
        compiler_params=_params(("parallel",)),
        name="shared_kv",
    )(x, w)


def _layer_norm(r, g, b):
    mu = jnp.mean(r, axis=-1, keepdims=True)
    c = r - mu
    var = jnp.mean(c * c, axis=-1, keepdims=True)
    return c * lax.rsqrt(var + LN_EPS) * g + b


def _memory_heads(qm_ref, mk_ref, mvt_ref, o_ref):
    tq = qm_ref.shape[1]
    step = min(tq, 2 * LANES)
    units = [(h, r0) for r0 in range(0, tq, step) for h in range(N_MEM_HEADS)]
    scores = [_dot_nt(mk_ref[0, :, _head(h)], qm_ref[0, r0:r0 + step, _head(h)]) * SCALE
              for h, r0 in units]
    probs, dens = [], []
    for s in scores:
        m = jnp.max(s, axis=0, keepdims=True)
        p = jnp.exp(s - m)
        dens.append(jnp.sum(p, axis=0, keepdims=True))
        probs.append(p.astype(BF16))
    outs = [_dot(mvt_ref[0, _head(h), :], p) for (h, _), p in zip(units, probs)]
    for (h, r0), ot, l in zip(units, outs, dens):
        o_ref[0, r0:r0 + step, Q_W + h * HEAD_DIM:Q_W + (h + 1) * HEAD_DIM] = (ot / l).T.astype(o_ref.dtype)


def _swa_kernel(sink_ref, q_ref, kp_ref, kc_ref, vtp_ref, vtc_ref, qm_ref, mk_ref, mvt_ref, o_ref):
    tq = q_ref.shape[1]
    n = pl.program_id(1)
    sw = WINDOW
    c_io = lax.broadcasted_iota(jnp.int32, (2 * sw, sw), 0)
    r_io = lax.broadcasted_iota(jnp.int32, (2 * sw, sw), 1)
    dist_i = r_io + sw - c_io
    band = (dist_i >= 0) & (dist_i < sw)
    dist = dist_i.astype(F32)
    first_row = jnp.where(n > 0, 0, sw)
    band_first = band & (c_io >= first_row)
    for sb in range(tq // sw):
        rows = slice(sb * sw, (sb + 1) * sw)
        valid = band_first if sb == 0 else band
        for kvh in range(N_KV_HEADS):
            cs = _head(kvh)
            if sb == 0:
                kw = jnp.concatenate([kp_ref[0, :, cs], kc_ref[0, :sw, cs]], axis=0)
                vtw = jnp.concatenate([vtp_ref[0, cs, :], vtc_ref[0, cs, :sw]], axis=1)
            else:
                kw = kc_ref[0, (sb - 1) * sw:(sb + 1) * sw, cs]
                vtw = vtc_ref[0, cs, (sb - 1) * sw:(sb + 1) * sw]
            heads = [kvh * GROUP + g for g in range(GROUP)]
            q = jnp.concatenate([q_ref[0, rows, _head(h)] for h in heads], axis=0)
            s_all = _dot_nt(kw, q) * SCALE
            ps, denoms = [], []
            for g, h in enumerate(heads):
                s = s_all[:, g * sw:(g + 1) * sw] - SLOPES[h] * dist
                s = jnp.where(valid, s, NEG_INF)
                sink = sink_ref[h]
                m = jnp.maximum(jnp.max(s, axis=0, keepdims=True), sink)
                p = jnp.exp(s - m)
                denoms.append(jnp.sum(p, axis=0, keepdims=True) + jnp.exp(sink - m))
                ps.append(p.astype(BF16))
            ot_all = _dot(vtw, jnp.concatenate(ps, axis=1))
            for g, h in enumerate(heads):
                ot = ot_all[:, g * sw:(g + 1) * sw] / denoms[g]
                o_ref[0, rows, _head(h)] = ot.T.astype(o_ref.dtype)
    _memory_heads(qm_ref, mk_ref, mvt_ref, o_ref)


def _swa_layer(proj, vt, mkv, mvt, sinks, tq):
    b, t, _ = proj.shape
    mlen = mkv.shape[1]
    wpb = tq // WINDOW
    kcol = Q_W // KV_W
    prev = lambda n: jnp.maximum(n * wpb - 1, 0)
    return pl.pallas_call(
        _swa_kernel,
        grid=(b, t // tq),
        in_specs=[
            pl.BlockSpec(memory_space=pltpu.SMEM),
            pl.BlockSpec((1, tq, Q_W), lambda bi, n: (bi, n, 0)),
            pl.BlockSpec((1, WINDOW, KV_W), lambda bi, n: (bi, prev(n), kcol)),
            pl.BlockSpec((1, tq, KV_W), lambda bi, n: (bi, n, kcol)),
            pl.BlockSpec((1, KV_W, WINDOW), lambda bi, n: (bi, 0, prev(n))),
            pl.BlockSpec((1, KV_W, tq), lambda bi, n: (bi, 0, n)),
            pl.BlockSpec((1, tq, MQ_W), lambda bi, n: (bi, n, kcol + 2)),
            pl.BlockSpec((1, mlen, MQ_W), lambda bi, n: (bi, 0, 0)),
            pl.BlockSpec((1, MQ_W, mlen), lambda bi, n: (bi, 0, 0)),
        ],
        out_specs=pl.BlockSpec((1, tq, MIX_W), lambda bi, n: (bi, n, 0)),
        out_shape=jax.ShapeDtypeStruct((b, t, MIX_W), BF16),
        compiler_params=_params(("parallel", "arbitrary")),
        name="swa_mem_attn",
    )(sinks, proj, proj, proj, vt, vt, proj, mkv, mvt)


def _moba_kernel(q_ref, k_ref, vt_ref, km_ref, qm_ref, mk_ref, mvt_ref, o_ref,
                 qst_ref, al_ref, sel_ref, m_ref, l_ref, acc_ref):
    tq = MOBA_BLOCK
    nb = k_ref.shape[1]
    gq = GROUP * tq
    n_chunks = gq // LANES
    per_head = tq // LANES
    j = pl.program_id(1)
    key = lax.broadcasted_iota(jnp.int32, (tq, tq), 0)
    qry = lax.broadcasted_iota(jnp.int32, (tq, tq), 1)
    causal = key <= qry
    s_minus_t = (key - qry).astype(F32)
    blk = lax.broadcasted_iota(jnp.int32, (nb, gq), 0)
    for kvh in range(N_KV_HEADS):
        cs = _head(kvh)
        heads = [kvh * GROUP + g for g in range(GROUP)]
        for g, h in enumerate(heads):
            qst_ref[g * tq:(g + 1) * tq, :] = q_ref[0, :, _head(h)]
            al_ref[g] = SLOPES[h] * s_minus_t
        slope_row = jnp.concatenate([jnp.full((1, tq), SLOPES[h], F32) for h in heads], axis=1)
        kmb = km_ref[0, :, cs].astype(BF16)
        gate = jnp.where(blk < j, _dot_nt(kmb, qst_ref[...]), NEG_INF)
        cnt = jnp.zeros((nb, gq), F32)
        for ip in range(nb):
            gi = gate[ip:ip + 1, :]
            beats = (gi > gate) | ((gi == gate) & (ip < blk))
            cnt = cnt + jnp.where(beats, 1.0, 0.0)
        sel = jnp.where((cnt < MOBA_TOPK) & (blk < j), 0.0, NEG_INF)
        for ip in range(nb):
            sel_ref[ip] = sel[ip:ip + 1, :]

        s_all = _dot_nt(k_ref[0, j, :, cs], qst_ref[...])
        ps = []
        for c in range(n_chunks):
            g, half = divmod(c, per_head)
            lanes, hl = _lane_chunk(c), _lane_chunk(half)
            s = jnp.where(causal[:, hl], s_all[:, lanes] * SCALE + al_ref[g, :, hl], NEG_INF)
            m0 = jnp.max(s, axis=0, keepdims=True)
            p = jnp.exp(s - m0)
            m_ref[:, lanes] = m0
            l_ref[:, lanes] = jnp.sum(p, axis=0, keepdims=True)
            ps.append(p.astype(BF16))
        acc_ref[...] = _dot(vt_ref[0, j, cs, :], jnp.concatenate(ps, axis=1))

        def past_block(i, carry):
            gap = ((j - i) * tq).astype(F32)
            rowc = sel_ref[i] - slope_row * gap
            s_all = _dot_nt(k_ref[0, i, :, cs], qst_ref[...])
            ps, scales = [], []
            for c in range(n_chunks):
                g, half = divmod(c, per_head)
                lanes, hl = _lane_chunk(c), _lane_chunk(half)
                s = s_all[:, lanes] * SCALE + al_ref[g, :, hl] + rowc[:, lanes]
                m_old = m_ref[:, lanes]
                m_new = jnp.maximum(m_old, jnp.max(s, axis=0, keepdims=True))
                a = jnp.exp(m_old - m_new)
                p = jnp.exp(s - m_new)
                l_ref[:, lanes] = a * l_ref[:, lanes] + jnp.sum(p, axis=0, keepdims=True)
                m_ref[:, lanes] = m_new
                ps.append(p.astype(BF16))
                scales.append(a)
            pv = _dot(vt_ref[0, i, cs, :], jnp.concatenate(ps, axis=1))
            acc_ref[...] = jnp.concatenate(scales, axis=1) * acc_ref[...] + pv
            return carry

        lax.fori_loop(0, j, past_block, 0)
        ot = acc_ref[...] / l_ref[...]
        for g, h in enumerate(heads):
            o_ref[0, :, _head(h)] = ot[:, g * tq:(g + 1) * tq].T.astype(o_ref.dtype)
    _memory_heads(qm_ref, mk_ref, mvt_ref, o_ref)


def _moba_layer(proj, k4, vt4, kmeans, mkv, mvt):
    b, t, _ = proj.shape
    mlen = mkv.shape[1]
    tq = MOBA_BLOCK
    nb = t // tq
    gq = GROUP * tq
    return pl.pallas_call(
        _moba_kernel,
        grid=(b, nb),
        in_specs=[
            pl.BlockSpec((1, tq, Q_W), lambda bi, n: (bi, n, 0)),
            pl.BlockSpec((1, nb, tq, KV_W), lambda bi, n: (bi, 0, 0, 0)),
            pl.BlockSpec((1, nb, KV_W, tq), lambda bi, n: (bi, 0, 0, 0)),
            pl.BlockSpec((1, nb, KV_W), lambda bi, n: (bi, 0, 0)),
            pl.BlockSpec((1, tq, MQ_W), lambda bi, n: (bi, n, Q_W // MQ_W)),
            pl.BlockSpec((1, mlen, MQ_W), lambda bi, n: (bi, 0, 0)),
            pl.BlockSpec((1, MQ_W, mlen), lambda bi, n: (bi, 0, 0)),
        ],
        out_specs=pl.BlockSpec((1, tq, MIX_W), lambda bi, n: (bi, n, 0)),
        out_shape=jax.ShapeDtypeStruct((b, t, MIX_W), BF16),
        scratch_shapes=[pltpu.VMEM((gq, HEAD_DIM), BF16),
                        pltpu.VMEM((GROUP, tq, tq), F32),
                        pltpu.VMEM((nb, 1, gq), F32),
                        pltpu.VMEM((1, gq), F32),
                        pltpu.VMEM((1, gq), F32),
                        pltpu.VMEM((HEAD_DIM, gq), F32)],
        compiler_params=_params(("parallel", "arbitrary")),
        name="moba_mem_attn",
    )(proj, k4, vt4, kmeans, proj, mkv, mvt)


def _out_proj_kernel(mix_ref, w_ref, h_ref, g_ref, b_ref, o_ref):
    r = ALPHA * h_ref[...] + _dot(mix_ref[...], w_ref[...])
    o_ref[...] = _layer_norm(r, g_ref[...], b_ref[...])


def _ln_specs(ln_index):
    spec = pl.BlockSpec((None, 1, D_MODEL), lambda *_: (ln_index, 0, 0))
    return [spec, spec]


def _out_proj_ln(mix, w, layer, h, ln_g, ln_b, tm):
    m, k = mix.shape
    n = w.shape[2]
    return pl.pallas_call(
        _out_proj_kernel,
        grid=(m // tm,),
        in_specs=[pl.BlockSpec((tm, k), lambda i: (i, 0)),
                  pl.BlockSpec((None, k, n), lambda i: (layer, 0, 0)),
                  pl.BlockSpec((tm, n), lambda i: (i, 0))] + _ln_specs(2 * layer),
        out_specs=pl.BlockSpec((tm, n), lambda i: (i, 0)),
        out_shape=jax.ShapeDtypeStruct((m, n), F32),
        compiler_params=_params(("parallel",)),
        name="out_proj_ln",
    )(mix, w, h, ln_g, ln_b)


def _ffn_kernel(h_ref, wu_ref, wd_ref, g_ref, b_ref, o_ref, hb_ref):
    f = pl.program_id(1)

    @pl.when(f == 0)
    def _():
        hb_ref[...] = h_ref[...].astype(BF16)
        o_ref[...] = ALPHA * h_ref[...]

    u = jnp.maximum(_dot(hb_ref[...], wu_ref[...].astype(BF16)), 0.0)
    o_ref[...] += _dot((u * u).astype(BF16), wd_ref[...].astype(BF16))

    @pl.when(f == pl.num_programs(1) - 1)
    def _():
        o_ref[...] = _layer_norm(o_ref[...], g_ref[...], b_ref[...])


def _ffn_ln(h, w_up, w_down, layer, ln_g, ln_b, tm, tf):
    m, d = h.shape
    dff = w_up.shape[2]
    return pl.pallas_call(
        _ffn_kernel,
        grid=(m // tm, dff // tf),
        in_specs=[pl.BlockSpec((tm, d), lambda i, f: (i, 0), pipeline_mode=pl.Buffered(1)),
                  pl.BlockSpec((None, d, tf), lambda i, f: (layer, 0, f)),
                  pl.BlockSpec((None, tf, d), lambda i, f: (layer, f, 0))] + _ln_specs(2 * layer + 1),
        out_specs=pl.BlockSpec((tm, d), lambda i, f: (i, 0)),
        out_shape=jax.ShapeDtypeStruct((m, d), F32),
        scratch_shapes=[pltpu.VMEM((tm, d), BF16)],
        compiler_params=_params(("parallel", "arbitrary")),
        name="ffn_ln",
    )(h, w_up, w_down, ln_g, ln_b)


PROJ_TM, PROJ_TN = 1024, 512
KV_TM = 512
OUT_TM = 512
FFN_TM, FFN_TF = 1024, 512
SWA_TQ = 512


def kernel(x, mem, w_in_a, sinks_a, w_q_b, w_kv_shared, w_mem_kv, w_o, w_up, w_down, ln_g, ln_b):
    b, t, d = x.shape
    mlen = mem.shape[1]
    m = b * t
    nb = t // MOBA_BLOCK
    h = x.reshape(m, d)
    mem2 = mem.reshape(b * mlen, d)
    w_o_bf = w_o.astype(BF16)
    ln_g2 = ln_g.reshape(2 * DEPTH, 1, d)
    ln_b2 = ln_b.reshape(2 * DEPTH, 1, d)
    k4 = vt4 = kmeans = None
    for layer in range(DEPTH):
        mkv = _matmul(mem2, w_mem_kv, layer, b * mlen, 512, "mem_kv")
        mkv = mkv.reshape(b, mlen, 2 * MQ_W)
        mvt = jnp.swapaxes(mkv[:, :, MQ_W:], 1, 2)
        if layer < N_A_LAYERS:
            proj = _matmul(h, w_in_a, layer, PROJ_TM, PROJ_TN, "proj_a").reshape(b, t, -1)
            vt = jnp.swapaxes(proj[:, :, Q_W + KV_W:Q_W + 2 * KV_W], 1, 2)
            mix = _swa_layer(proj, vt, mkv, mvt, sinks_a[layer], SWA_TQ)
        else:
            if k4 is None:
                kv, km = _shared_kv(h, w_kv_shared, KV_TM)
                k4 = kv.reshape(b, nb, MOBA_BLOCK, 2 * KV_W)
                vt4 = jnp.swapaxes(k4[..., KV_W:], 2, 3)
                kmeans = km.reshape(b, nb, KV_W)
            proj = _matmul(h, w_q_b, layer - N_A_LAYERS, PROJ_TM, PROJ_TN, "proj_b")
            mix = _moba_layer(proj.reshape(b, t, -1), k4, vt4, kmeans, mkv, mvt)
        h = _out_proj_ln(mix.reshape(m, MIX_W), w_o_bf, layer, h, ln_g2, ln_b2, OUT_TM)
        h = _ffn_ln(h, w_up, w_down, layer, ln_g2, ln_b2, FFN_TM, FFN_TF)
    return h.reshape(b, t, d)
```

```python
import math

import jax
import jax.numpy as jnp
from jax import lax
from jax.experimental import pallas as pl
from jax.experimental.pallas import tpu as pltpu

D_MODEL = 2048
DEPTH = 4
HEAD_DIM = 128
N_Q_HEADS = 12
N_KV_HEADS = 4
GROUP = N_Q_HEADS // N_KV_HEADS
N_MEM_HEADS = 4
WINDOW = 128
MOBA_BLOCK = 256
MOBA_TOPK = 3
D_FF = 4 * D_MODEL
N_A_LAYERS = DEPTH // 2
ALPHA = (2 * DEPTH) ** 0.25
LN_EPS = 1e-5
NEG_INF = -1e30
Q_W = N_Q_HEADS * HEAD_DIM
KV_W = N_KV_HEADS * HEAD_DIM
MQ_W = N_MEM_HEADS * HEAD_DIM
MIX_W = Q_W + MQ_W
SCALE = HEAD_DIM ** -0.5
SLOPES = tuple(2.0 ** (-8.0 * h / N_Q_HEADS) for h in range(1, N_Q_HEADS + 1))
LOG2E = math.log2(math.e)
SCALE2 = SCALE * LOG2E
SLOPES2 = tuple(s * LOG2E for s in SLOPES)

V7X_VMEM_BYTES = 64 * 1024 * 1024
VMEM_LIMIT = V7X_VMEM_BYTES - 8 * 1024 * 1024
LANES = 128

BF16 = jnp.bfloat16
F32 = jnp.float32


def _params(sem):
    return pltpu.CompilerParams(dimension_semantics=sem, vmem_limit_bytes=VMEM_LIMIT)


def _dot(a, b):
    return jnp.dot(a, b, preferred_element_type=F32)


def _dot_nt(a, b):
    return lax.dot_general(a, b, (((1,), (1,)), ((), ())), preferred_element_type=F32)


def _head(h):
    return slice(h * HEAD_DIM, (h + 1) * HEAD_DIM)


def _lane_chunk(c):
    return slice(c * LANES, (c + 1) * LANES)


def _mm_kernel(x_ref, w_ref, o_ref, xb_ref):
    @pl.when(pl.program_id(1) == 0)
    def _():
        xb_ref[...] = x_ref[...].astype(BF16)

    o_ref[...] = _dot(xb_ref[...], w_ref[...].astype(BF16)).astype(o_ref.dtype)


def _matmul(x, w, layer, tm, tn, name):
    m, k = x.shape
    n = w.shape[2]
    return pl.pallas_call(
        _mm_kernel,
        grid=(m // tm, n // tn),
        in_specs=[pl.BlockSpec((tm, k), lambda i, j: (i, 0)),
                  pl.BlockSpec((None, k, tn), lambda i, j: (layer, 0, j))],
        out_specs=pl.BlockSpec((tm, tn), lambda i, j: (i, j)),
        out_shape=jax.ShapeDtypeStruct((m, n), BF16),
        scratch_shapes=[pltpu.VMEM((tm, k), BF16)],
        compiler_params=_params(("parallel", "arbitrary")),
        name=name,
    )(x, w)


def _kv_kernel(x_ref, w_ref, kv_ref, km_ref):
    acc = _dot(x_ref[...].astype(BF16), w_ref[...].astype(BF16))
    kv_ref[...] = acc.astype(kv_ref.dtype)
    nblk = acc.shape[0] // MOBA_BLOCK
    k = acc[:, :KV_W].reshape(nblk, MOBA_BLOCK, KV_W)
    km_ref[...] = jnp.mean(k, axis=1)[:, None, :]


def _shared_kv(x, w, tm):
    m, k = x.shape
    n = w.shape[1]
    nblk = tm // MOBA_BLOCK
    return pl.pallas_call(
        _kv_kernel,
        grid=(m // tm,),
        in_specs=[pl.BlockSpec((tm, k), lambda i: (i, 0)),
                  pl.BlockSpec((k, n), lambda i: (0, 0))],
        out_specs=[pl.BlockSpec((tm, n), lambda i: (i, 0)),
                   pl.BlockSpec((nblk, 1, KV_W), lambda i: (i, 0, 0))],
        out_shape=[jax.ShapeDtypeStruct((m, n), BF16),
                   jax.ShapeDtypeStruct((m // MOBA_BLOCK, 1, KV_W), F32)],
        compiler_params=_params(("parallel",)),
        name="shared_kv",
    )(x, w)


def _layer_norm(r, g, b):
    mu = jnp.mean(r, axis=-1, keepdims=True)
    c = r - mu
    var = jnp.mean(c * c, axis=-1, keepdims=True)
    return c * lax.rsqrt(var + LN_EPS) * g + b


def _memory_heads(qm_ref, mk_ref, mvt_ref, o_ref):
    tq = qm_ref.shape[1]
    step = min(tq, 2 * LANES)
    units = [(h, r0) for r0 in range(0, tq, step) for h in range(N_MEM_HEADS)]
    scores = [_dot_nt(mk_ref[0, :, _head(h)], qm_ref[0, r0:r0 + step, _head(h)]) * SCALE2
              for h, r0 in units]
    probs, dens = [], []
    for s in scores:
        m = jnp.max(s, axis=0, keepdims=True)
        p = jnp.exp2(s - m)
        dens.append(jnp.sum(p, axis=0, keepdims=True))
        probs.append(p.astype(BF16))
    outs = [_dot(mvt_ref[0, _head(h), :], p) for (h, _), p in zip(units, probs)]
    for (h, r0), ot, l in zip(units, outs, dens):
        o_ref[0, r0:r0 + step, Q_W + h * HEAD_DIM:Q_W + (h + 1) * HEAD_DIM] = (ot / l).T.astype(o_ref.dtype)


def _swa_kernel(sink_ref, q_ref, kp_ref, kc_ref, vtp_ref, vtc_ref, qm_ref, mk_ref, mvt_ref, o_ref):
    tq = q_ref.shape[1]
    n = pl.program_id(1)
    sw = WINDOW
    c_io = lax.broadcasted_iota(jnp.int32, (2 * sw, sw), 0)
    r_io = lax.broadcasted_iota(jnp.int32, (2 * sw, sw), 1)
    dist_i = r_io + sw - c_io
    band = (dist_i >= 0) & (dist_i < sw)
    neg_dist = jnp.where(band, -dist_i.astype(F32), NEG_INF)
    first_row = jnp.where(n > 0, 0, sw)
    neg_dist_first = jnp.where(c_io >= first_row, neg_dist, NEG_INF)
    for sb in range(tq // sw):
        rows = slice(sb * sw, (sb + 1) * sw)
        nd = neg_dist_first if sb == 0 else neg_dist
        scores, vtws = [], []
        for kvh in range(N_KV_HEADS):
            cs = _head(kvh)
            if sb == 0:
                kw = jnp.concatenate([kp_ref[0, :, cs], kc_ref[0, :sw, cs]], axis=0)
                vtw = jnp.concatenate([vtp_ref[0, cs, :], vtc_ref[0, cs, :sw]], axis=1)
            else:
                kw = kc_ref[0, (sb - 1) * sw:(sb + 1) * sw, cs]
                vtw = vtc_ref[0, cs, (sb - 1) * sw:(sb + 1) * sw]
            q = jnp.concatenate([q_ref[0, rows, _head(kvh * GROUP + g)] for g in range(GROUP)], axis=0)
            scores.append(_dot_nt(kw, q))
            vtws.append(vtw)
        probs, denoms = [], []
        for h in range(N_Q_HEADS):
            kvh, g = divmod(h, GROUP)
            s = scores[kvh][:, g * sw:(g + 1) * sw] * SCALE2 + SLOPES2[h] * nd
            sink = sink_ref[h] * LOG2E
            m = jnp.maximum(jnp.max(s, axis=0, keepdims=True), sink)
            p = jnp.exp2(s - m)
            denoms.append(jnp.sum(p, axis=0, keepdims=True) + jnp.exp2(sink - m))
            probs.append(p.astype(BF16))
        outs = [_dot(vtws[kvh], jnp.concatenate(probs[kvh * GROUP:(kvh + 1) * GROUP], axis=1))
                for kvh in range(N_KV_HEADS)]
        for h in range(N_Q_HEADS):
            kvh, g = divmod(h, GROUP)
            ot = outs[kvh][:, g * sw:(g + 1) * sw] / denoms[h]
            o_ref[0, rows, _head(h)] = ot.T.astype(o_ref.dtype)
    _memory_heads(qm_ref, mk_ref, mvt_ref, o_ref)


def _swa_layer(proj, vt, mkv, mvt, sinks, tq):
    b, t, _ = proj.shape
    mlen = mkv.shape[1]
    wpb = tq // WINDOW
    kcol = Q_W // KV_W
    prev = lambda n: jnp.maximum(n * wpb - 1, 0)
    return pl.pallas_call(
        _swa_kernel,
        grid=(b, t // tq),
        in_specs=[
            pl.BlockSpec(memory_space=pltpu.SMEM),
            pl.BlockSpec((1, tq, Q_W), lambda bi, n: (bi, n, 0)),
            pl.BlockSpec((1, WINDOW, KV_W), lambda bi, n: (bi, prev(n), kcol)),
            pl.BlockSpec((1, tq, KV_W), lambda bi, n: (bi, n, kcol)),
            pl.BlockSpec((1, KV_W, WINDOW), lambda bi, n: (bi, 0, prev(n))),
            pl.BlockSpec((1, KV_W, tq), lambda bi, n: (bi, 0, n)),
            pl.BlockSpec((1, tq, MQ_W), lambda bi, n: (bi, n, kcol + 2)),
            pl.BlockSpec((1, mlen, MQ_W), lambda bi, n: (bi, 0, 0)),
            pl.BlockSpec((1, MQ_W, mlen), lambda bi, n: (bi, 0, 0)),
        ],
        out_specs=pl.BlockSpec((1, tq, MIX_W), lambda bi, n: (bi, n, 0)),
        out_shape=jax.ShapeDtypeStruct((b, t, MIX_W), BF16),
        compiler_params=_params(("parallel", "arbitrary")),
        name="swa_mem_attn",
    )(sinks, proj, proj, proj, vt, vt, proj, mkv, mvt)


def _moba_kernel(q_ref, k_ref, vt_ref, km_ref, qm_ref, mk_ref, mvt_ref, o_ref,
                 al_ref, sel_ref, m_ref, l_ref, acc_ref):
    tq = MOBA_BLOCK
    nb = k_ref.shape[1]
    per_head = tq // LANES
    j = pl.program_id(1)
    key = lax.broadcasted_iota(jnp.int32, (tq, tq), 0)
    qry = lax.broadcasted_iota(jnp.int32, (tq, tq), 1)
    s_minus_t = (key - qry).astype(F32)
    causal_add = jnp.where(key <= qry, 0.0, NEG_INF)
    blk = lax.broadcasted_iota(jnp.int32, (nb, tq), 0)
    heads = range(N_Q_HEADS)

    def head_lanes(h):
        return slice(h * tq, (h + 1) * tq)

    def scores_of(i):
        return [_dot_nt(k_ref[0, i, :, _head(h // GROUP)], q_ref[0, :, _head(h)]) for h in heads]

    def values_of(i, probs):
        return [_dot(vt_ref[0, i, _head(h // GROUP), :], probs[h]) for h in heads]

    for h in heads:
        kmb = km_ref[0, :, _head(h // GROUP)].astype(BF16)
        gate = jnp.where(blk < j, _dot_nt(kmb, q_ref[0, :, _head(h)]), NEG_INF)
        cnt = jnp.zeros((nb, tq), F32)
        for ip in range(nb):
            gi = gate[ip:ip + 1, :]
            beats = (gi > gate) | ((gi == gate) & (ip < blk))
            cnt = cnt + jnp.where(beats, 1.0, 0.0)
        sel = jnp.where((cnt < MOBA_TOPK) & (blk < j), 0.0, NEG_INF)
        for ip in range(nb):
            sel_ref[ip, :, head_lanes(h)] = sel[ip:ip + 1, :]
        al_ref[h] = SLOPES2[h] * s_minus_t
    slope_row = jnp.concatenate([jnp.full((1, tq), SLOPES2[h], F32) for h in heads], axis=1)

    raw = scores_of(j)
    probs = []
    for h in heads:
        ps = []
        for half in range(per_head):
            lanes, hl = _lane_chunk(h * per_head + half), _lane_chunk(half)
            s = raw[h][:, hl] * SCALE2 + (al_ref[h, :, hl] + causal_add[:, hl])
            m0 = jnp.max(s, axis=0, keepdims=True)
            p = jnp.exp2(s - m0)
            m_ref[:, lanes] = m0
            l_ref[:, lanes] = jnp.sum(p, axis=0, keepdims=True)
            ps.append(p.astype(BF16))
        probs.append(jnp.concatenate(ps, axis=1))
    for h, pv in zip(heads, values_of(j, probs)):
        acc_ref[:, head_lanes(h)] = pv

    def past_block(i, carry):
        gap = ((j - i) * tq).astype(F32)
        rowc = sel_ref[i] - slope_row * gap
        raw = scores_of(i)
        probs, scales = [], []
        for h in heads:
            ps, sc = [], []
            for half in range(per_head):
                lanes, hl = _lane_chunk(h * per_head + half), _lane_chunk(half)
                s = raw[h][:, hl] * SCALE2 + al_ref[h, :, hl] + rowc[:, lanes]
                m_old = m_ref[:, lanes]
                m_new = jnp.maximum(m_old, jnp.max(s, axis=0, keepdims=True))
                a = jnp.exp2(m_old - m_new)
                p = jnp.exp2(s - m_new)
                l_ref[:, lanes] = a * l_ref[:, lanes] + jnp.sum(p, axis=0, keepdims=True)
                m_ref[:, lanes] = m_new
                ps.append(p.astype(BF16))
                sc.append(a)
            probs.append(jnp.concatenate(ps, axis=1))
            scales.append(jnp.concatenate(sc, axis=1))
        for h, pv in zip(heads, values_of(i, probs)):
            acc_ref[:, head_lanes(h)] = scales[h] * acc_ref[:, head_lanes(h)] + pv
        return carry

    lax.fori_loop(0, j, past_block, 0)
    for h in heads:
        ot = acc_ref[:, head_lanes(h)] / l_ref[:, head_lanes(h)]
        o_ref[0, :, _head(h)] = ot.T.astype(o_ref.dtype)
    _memory_heads(qm_ref, mk_ref, mvt_ref, o_ref)


def _moba_layer(proj, k4, vt4, kmeans, mkv, mvt):
    b, t, _ = proj.shape
    mlen = mkv.shape[1]
    tq = MOBA_BLOCK
    nb = t // tq
    hq = N_Q_HEADS * tq
    return pl.pallas_call(
        _moba_kernel,
        grid=(b, nb),
        in_specs=[
            pl.BlockSpec((1, tq, Q_W), lambda bi, n: (bi, n, 0)),
            pl.BlockSpec((1, nb, tq, KV_W), lambda bi, n: (bi, 0, 0, 0)),
            pl.BlockSpec((1, nb, KV_W, tq), lambda bi, n: (bi, 0, 0, 0)),
            pl.BlockSpec((1, nb, KV_W), lambda bi, n: (bi, 0, 0)),
            pl.BlockSpec((1, tq, MQ_W), lambda bi, n: (bi, n, Q_W // MQ_W)),
            pl.BlockSpec((1, mlen, MQ_W), lambda bi, n: (bi, 0, 0)),
            pl.BlockSpec((1, MQ_W, mlen), lambda bi, n: (bi, 0, 0)),
        ],
        out_specs=pl.BlockSpec((1, tq, MIX_W), lambda bi, n: (bi, n, 0)),
        out_shape=jax.ShapeDtypeStruct((b, t, MIX_W), BF16),
        scratch_shapes=[pltpu.VMEM((N_Q_HEADS, tq, tq), F32),
                        pltpu.VMEM((nb, 1, hq), F32),
                        pltpu.VMEM((1, hq), F32),
                        pltpu.VMEM((1, hq), F32),
                        pltpu.VMEM((HEAD_DIM, hq), F32)],
        compiler_params=_params(("parallel", "arbitrary")),
        name="moba_mem_attn",
    )(proj, k4, vt4, kmeans, proj, mkv, mvt)


def _out_proj_kernel(mix_ref, w_ref, h_ref, g_ref, b_ref, o_ref):
    r = ALPHA * h_ref[...] + _dot(mix_ref[...], w_ref[...])
    o_ref[...] = _layer_norm(r, g_ref[...], b_ref[...])


def _ln_specs(ln_index):
    spec = pl.BlockSpec((None, 1, D_MODEL), lambda *_: (ln_index, 0, 0))
    return [spec, spec]


def _out_proj_ln(mix, w, layer, h, ln_g, ln_b, tm):
    m, k = mix.shape
    n = w.shape[2]
    return pl.pallas_call(
        _out_proj_kernel,
        grid=(m // tm,),
        in_specs=[pl.BlockSpec((tm, k), lambda i: (i, 0)),
                  pl.BlockSpec((None, k, n), lambda i: (layer, 0, 0)),
                  pl.BlockSpec((tm, n), lambda i: (i, 0))] + _ln_specs(2 * layer),
        out_specs=pl.BlockSpec((tm, n), lambda i: (i, 0)),
        out_shape=jax.ShapeDtypeStruct((m, n), F32),
        compiler_params=_params(("parallel",)),
        name="out_proj_ln",
    )(mix, w, h, ln_g, ln_b)


def _ffn_kernel(h_ref, wu_ref, wd_ref, g_ref, b_ref, o_ref, hb_ref):
    f = pl.program_id(1)

    @pl.when(f == 0)
    def _():
        hb_ref[...] = h_ref[...].astype(BF16)
        o_ref[...] = ALPHA * h_ref[...]

    u = jnp.maximum(_dot(hb_ref[...], wu_ref[...].astype(BF16)), 0.0)
    o_ref[...] += _dot((u * u).astype(BF16), wd_ref[...].astype(BF16))

    @pl.when(f == pl.num_programs(1) - 1)
    def _():
        o_ref[...] = _layer_norm(o_ref[...], g_ref[...], b_ref[...])


def _ffn_ln(h, w_up, w_down, layer, ln_g, ln_b, tm, tf):
    m, d = h.shape
    dff = w_up.shape[2]
    return pl.pallas_call(
        _ffn_kernel,
        grid=(m // tm, dff // tf),
        in_specs=[pl.BlockSpec((tm, d), lambda i, f: (i, 0), pipeline_mode=pl.Buffered(1)),
                  pl.BlockSpec((None, d, tf), lambda i, f: (layer, 0, f)),
                  pl.BlockSpec((None, tf, d), lambda i, f: (layer, f, 0))] + _ln_specs(2 * layer + 1),
        out_specs=pl.BlockSpec((tm, d), lambda i, f: (i, 0)),
        out_shape=jax.ShapeDtypeStruct((m, d), F32),
        scratch_shapes=[pltpu.VMEM((tm, d), BF16)],
        compiler_params=_params(("parallel", "arbitrary")),
        name="ffn_ln",
    )(h, w_up, w_down, ln_g, ln_b)


PROJ_TM, PROJ_TN = 1024, 512
KV_TM = 512
OUT_TM = 512
FFN_TM, FFN_TF = 1024, 512
SWA_TQ = 512


def kernel(x, mem, w_in_a, sinks_a, w_q_b, w_kv_shared, w_mem_kv, w_o, w_up, w_down, ln_g, ln_b):
    b, t, d = x.shape
    mlen = mem.shape[1]
    m = b * t
    nb = t // MOBA_BLOCK
    h = x.reshape(m, d)
    mem2 = mem.reshape(b * mlen, d)
    w_o_bf = w_o.astype(BF16)
    ln_g2 = ln_g.reshape(2 * DEPTH, 1, d)
    ln_b2 = ln_b.reshape(2 * DEPTH, 1, d)
    k4 = vt4 = kmeans = None
    for layer in range(DEPTH):
        mkv = _matmul(mem2, w_mem_kv, layer, b * mlen, 512, "mem_kv")
        mkv = mkv.reshape(b, mlen, 2 * MQ_W)
        mvt = jnp.swapaxes(mkv[:, :, MQ_W:], 1, 2)
        if layer < N_A_LAYERS:
            proj = _matmul(h, w_in_a, layer, PROJ_TM, PROJ_TN, "proj_a").reshape(b, t, -1)
            vt = jnp.swapaxes(proj[:, :, Q_W + KV_W:Q_W + 2 * KV_W], 1, 2)
            mix = _swa_layer(proj, vt, mkv, mvt, sinks_a[layer], SWA_TQ)
        else:
            if k4 is None:
                kv, km = _shared_kv(h, w_kv_shared, KV_TM)
                k4 = kv.reshape(b, nb, MOBA_BLOCK, 2 * KV_W)
                vt4 = jnp.swapaxes(k4[..., KV_W:], 2, 3)
                kmeans = km.reshape(b, nb, KV_W)
            proj = _matmul(h, w_q_b, layer - N_A_LAYERS, PROJ_TM, PROJ_TN, "proj_b")
            mix = _moba_layer(proj.reshape(b, t, -1), k4, vt4, kmeans, mkv, mvt)
        h = _out_proj_ln(mix.reshape(m, MIX_W), w_o_bf, layer, h, ln_g2, ln_b2, OUT_TM)
        h = _ffn_ln(h, w_up, w_down, layer, ln_g2, ln_b2, FFN_TM, FFN_TF)
    return h.reshape(b, t, d)
```

```python
import math

import jax
import jax.numpy as jnp
from jax import lax
from jax.experimental import pallas as pl
from jax.experimental.pallas import tpu as pltpu

D_MODEL = 2048
DEPTH = 4
HEAD_DIM = 128
N_Q_HEADS = 12
N_KV_HEADS = 4
GROUP = N_Q_HEADS // N_KV_HEADS
N_MEM_HEADS = 4
WINDOW = 128
MOBA_BLOCK = 256
MOBA_TOPK = 3
D_FF = 4 * D_MODEL
N_A_LAYERS = DEPTH // 2
ALPHA = (2 * DEPTH) ** 0.25
LN_EPS = 1e-5
NEG_INF = -1e30
Q_W = N_Q_HEADS * HEAD_DIM
KV_W = N_KV_HEADS * HEAD_DIM
MQ_W = N_MEM_HEADS * HEAD_DIM
MIX_W = Q_W + MQ_W
SCALE = HEAD_DIM ** -0.5
SLOPES = tuple(2.0 ** (-8.0 * h / N_Q_HEADS) for h in range(1, N_Q_HEADS + 1))
LOG2E = math.log2(math.e)
SCALE2 = SCALE * LOG2E
SLOPES2 = tuple(s * LOG2E for s in SLOPES)

V7X_VMEM_BYTES = 64 * 1024 * 1024
VMEM_LIMIT = V7X_VMEM_BYTES - 8 * 1024 * 1024
LANES = 128

BF16 = jnp.bfloat16
F32 = jnp.float32


def _params(sem):
    return pltpu.CompilerParams(dimension_semantics=sem, vmem_limit_bytes=VMEM_LIMIT)


def _dot(a, b):
    return jnp.dot(a, b, preferred_element_type=F32)


def _dot_nt(a, b):
    return lax.dot_general(a, b, (((1,), (1,)), ((), ())), preferred_element_type=F32)


def _head(h):
    return slice(h * HEAD_DIM, (h + 1) * HEAD_DIM)


def _lane_chunk(c):
    return slice(c * LANES, (c + 1) * LANES)


def _mm_kernel_f32(x_ref, w_ref, o_ref, xb_ref):
    @pl.when(pl.program_id(1) == 0)
    def _():
        xb_ref[...] = x_ref[...].astype(BF16)

    o_ref[...] = _dot(xb_ref[...], w_ref[...].astype(BF16)).astype(o_ref.dtype)


def _mm_kernel_bf16(x_ref, w_ref, o_ref):
    o_ref[...] = _dot(x_ref[...], w_ref[...].astype(BF16)).astype(o_ref.dtype)


def _matmul(x, w, layer, tm, tn, name):
    m, k = x.shape
    n = w.shape[2]
    is_f32 = x.dtype == F32
    return pl.pallas_call(
        _mm_kernel_f32 if is_f32 else _mm_kernel_bf16,
        grid=(m // tm, n // tn),
        in_specs=[pl.BlockSpec((tm, k), lambda i, j: (i, 0)),
                  pl.BlockSpec((None, k, tn), lambda i, j: (layer, 0, j))],
        out_specs=pl.BlockSpec((tm, tn), lambda i, j: (i, j)),
        out_shape=jax.ShapeDtypeStruct((m, n), BF16),
        scratch_shapes=[pltpu.VMEM((tm, k), BF16)] if is_f32 else [],
        compiler_params=_params(("parallel", "arbitrary")),
        name=name,
    )(x, w)


def _kv_kernel(x_ref, w_ref, kv_ref, km_ref):
    acc = _dot(x_ref[...].astype(BF16), w_ref[...].astype(BF16))
    kv_ref[...] = acc.astype(kv_ref.dtype)
    nblk = acc.shape[0] // MOBA_BLOCK
    k = acc[:, :KV_W].reshape(nblk, MOBA_BLOCK, KV_W)
    km_ref[...] = jnp.mean(k, axis=1)[:, None, :]


def _shared_kv(x, w, tm):
    m, k = x.shape
    n = w.shape[1]
    nblk = tm // MOBA_BLOCK
    return pl.pallas_call(
        _kv_kernel,
        grid=(m // tm,),
        in_specs=[pl.BlockSpec((tm, k), lambda i: (i, 0)),
                  pl.BlockSpec((k, n), lambda i: (0, 0))],
        out_specs=[pl.BlockSpec((tm, n), lambda i: (i, 0)),
                   pl.BlockSpec((nblk, 1, KV_W), lambda i: (i, 0, 0))],
        out_shape=[jax.ShapeDtypeStruct((m, n), BF16),
                   jax.ShapeDtypeStruct((m // MOBA_BLOCK, 1, KV_W), F32)],
        compiler_params=_params(("parallel",)),
        name="shared_kv",
    )(x, w)


def _layer_norm(r, g, b):
    mu = jnp.mean(r, axis=-1, keepdims=True)
    c = r - mu
    var = jnp.mean(c * c, axis=-1, keepdims=True)
    return c * lax.rsqrt(var + LN_EPS) * g + b


def _memory_heads(qm_ref, mk_ref, mvt_ref, o_ref):
    tq = qm_ref.shape[1]
    step = min(tq, 2 * LANES)
    units = [(h, r0) for r0 in range(0, tq, step) for h in range(N_MEM_HEADS)]
    scores = [_dot_nt(mk_ref[0, :, _head(h)], qm_ref[0, r0:r0 + step, _head(h)]) * SCALE2
              for h, r0 in units]
    probs, dens = [], []
    for s in scores:
        m = jnp.max(s, axis=0, keepdims=True)
        p = jnp.exp2(s - m)
        dens.append(jnp.sum(p, axis=0, keepdims=True))
        probs.append(p.astype(BF16))
    outs = [_dot(mvt_ref[0, _head(h), :], p) for (h, _), p in zip(units, probs)]
    for (h, r0), ot, l in zip(units, outs, dens):
        o_ref[0, r0:r0 + step, Q_W + h * HEAD_DIM:Q_W + (h + 1) * HEAD_DIM] = (ot / l).T.astype(o_ref.dtype)


def _swa_kernel(sink_ref, q_ref, kp_ref, kc_ref, vtp_ref, vtc_ref, qm_ref, mk_ref, mvt_ref, o_ref):
    tq = q_ref.shape[1]
    n = pl.program_id(1)
    sw = WINDOW
    c_io = lax.broadcasted_iota(jnp.int32, (2 * sw, sw), 0)
    r_io = lax.broadcasted_iota(jnp.int32, (2 * sw, sw), 1)
    dist_i = r_io + sw - c_io
    band = (dist_i >= 0) & (dist_i < sw)
    neg_dist = jnp.where(band, -dist_i.astype(F32), NEG_INF)
    first_row = jnp.where(n > 0, 0, sw)
    neg_dist_first = jnp.where(c_io >= first_row, neg_dist, NEG_INF)
    for sb in range(tq // sw):
        rows = slice(sb * sw, (sb + 1) * sw)
        nd = neg_dist_first if sb == 0 else neg_dist
        scores, vtws = [], []
        for kvh in range(N_KV_HEADS):
            cs = _head(kvh)
            if sb == 0:
                kw = jnp.concatenate([kp_ref[0, :, cs], kc_ref[0, :sw, cs]], axis=0)
                vtw = jnp.concatenate([vtp_ref[0, cs, :], vtc_ref[0, cs, :sw]], axis=1)
            else:
                kw = kc_ref[0, (sb - 1) * sw:(sb + 1) * sw, cs]
                vtw = vtc_ref[0, cs, (sb - 1) * sw:(sb + 1) * sw]
            q = jnp.concatenate([q_ref[0, rows, _head(kvh * GROUP + g)] for g in range(GROUP)], axis=0)
            scores.append(_dot_nt(kw, q))
            vtws.append(vtw)
        probs, denoms = [], []
        for h in range(N_Q_HEADS):
            kvh, g = divmod(h, GROUP)
            s = scores[kvh][:, g * sw:(g + 1) * sw] * SCALE2 + SLOPES2[h] * nd
            sink = sink_ref[h] * LOG2E
            m = jnp.maximum(jnp.max(s, axis=0, keepdims=True), sink)
            p = jnp.exp2(s - m)
            denoms.append(jnp.sum(p, axis=0, keepdims=True) + jnp.exp2(sink - m))
            probs.append(p.astype(BF16))
        outs = [_dot(vtws[kvh], jnp.concatenate(probs[kvh * GROUP:(kvh + 1) * GROUP], axis=1))
                for kvh in range(N_KV_HEADS)]
        for h in range(N_Q_HEADS):
            kvh, g = divmod(h, GROUP)
            ot = outs[kvh][:, g * sw:(g + 1) * sw] / denoms[h]
            o_ref[0, rows, _head(h)] = ot.T.astype(o_ref.dtype)
    _memory_heads(qm_ref, mk_ref, mvt_ref, o_ref)


def _swa_layer(proj, vt, mkv, mvt, sinks, tq):
    b, t, _ = proj.shape
    mlen = mkv.shape[1]
    wpb = tq // WINDOW
    kcol = Q_W // KV_W
    prev = lambda n: jnp.maximum(n * wpb - 1, 0)
    return pl.pallas_call(
        _swa_kernel,
        grid=(b, t // tq),
        in_specs=[
            pl.BlockSpec(memory_space=pltpu.SMEM),
            pl.BlockSpec((1, tq, Q_W), lambda bi, n: (bi, n, 0)),
            pl.BlockSpec((1, WINDOW, KV_W), lambda bi, n: (bi, prev(n), kcol)),
            pl.BlockSpec((1, tq, KV_W), lambda bi, n: (bi, n, kcol)),
            pl.BlockSpec((1, KV_W, WINDOW), lambda bi, n: (bi, 0, prev(n))),
            pl.BlockSpec((1, KV_W, tq), lambda bi, n: (bi, 0, n)),
            pl.BlockSpec((1, tq, MQ_W), lambda bi, n: (bi, n, kcol + 2)),
            pl.BlockSpec((1, mlen, MQ_W), lambda bi, n: (bi, 0, 0)),
            pl.BlockSpec((1, MQ_W, mlen), lambda bi, n: (bi, 0, 0)),
        ],
        out_specs=pl.BlockSpec((1, tq, MIX_W), lambda bi, n: (bi, n, 0)),
        out_shape=jax.ShapeDtypeStruct((b, t, MIX_W), BF16),
        compiler_params=_params(("parallel", "arbitrary")),
        name="swa_mem_attn",
    )(sinks, proj, proj, proj, vt, vt, proj, mkv, mvt)


def _moba_kernel(q_ref, k_ref, vt_ref, km_ref, qm_ref, mk_ref, mvt_ref, o_ref,
                 al_ref, sel_ref, m_ref, l_ref, acc_ref):
    tq = MOBA_BLOCK
    nb = k_ref.shape[1]
    per_head = tq // LANES
    j = pl.program_id(1)
    key = lax.broadcasted_iota(jnp.int32, (tq, tq), 0)
    qry = lax.broadcasted_iota(jnp.int32, (tq, tq), 1)
    s_minus_t = (key - qry).astype(F32)
    causal_add = jnp.where(key <= qry, 0.0, NEG_INF)
    blk = lax.broadcasted_iota(jnp.int32, (nb, tq), 0)
    heads = range(N_Q_HEADS)

    def head_lanes(h):
        return slice(h * tq, (h + 1) * tq)

    def scores_of(i):
        return [_dot_nt(k_ref[0, i, :, _head(h // GROUP)], q_ref[0, :, _head(h)]) for h in heads]

    def values_of(i, probs):
        return [_dot(vt_ref[0, i, _head(h // GROUP), :], probs[h]) for h in heads]

    for h in heads:
        kmb = km_ref[0, :, _head(h // GROUP)].astype(BF16)
        gate = jnp.where(blk < j, _dot_nt(kmb, q_ref[0, :, _head(h)]), NEG_INF)
        cnt = jnp.zeros((nb, tq), F32)
        for ip in range(nb):
            gi = gate[ip:ip + 1, :]
            beats = (gi > gate) | ((gi == gate) & (ip < blk))
            cnt = cnt + jnp.where(beats, 1.0, 0.0)
        sel = jnp.where((cnt < MOBA_TOPK) & (blk < j), 0.0, NEG_INF)
        for ip in range(nb):
            sel_ref[ip, :, head_lanes(h)] = sel[ip:ip + 1, :]
        al_ref[h] = SLOPES2[h] * s_minus_t
    slope_row = jnp.concatenate([jnp.full((1, tq), SLOPES2[h], F32) for h in heads], axis=1)

    raw = scores_of(j)
    probs = []
    for h in heads:
        ps = []
        for half in range(per_head):
            lanes, hl = _lane_chunk(h * per_head + half), _lane_chunk(half)
            s = raw[h][:, hl] * SCALE2 + (al_ref[h, :, hl] + causal_add[:, hl])
            m0 = jnp.max(s, axis=0, keepdims=True)
            p = jnp.exp2(s - m0)
            m_ref[:, lanes] = m0
            l_ref[:, lanes] = jnp.sum(p, axis=0, keepdims=True)
            ps.append(p.astype(BF16))
        probs.append(jnp.concatenate(ps, axis=1))
    for h, pv in zip(heads, values_of(j, probs)):
        acc_ref[:, head_lanes(h)] = pv

    def past_block(i, carry):
        gap = ((j - i) * tq).astype(F32)
        rowc = sel_ref[i] - slope_row * gap
        raw = scores_of(i)
        probs, scales = [], []
        for h in heads:
            ps, sc = [], []
            for half in range(per_head):
                lanes, hl = _lane_chunk(h * per_head + half), _lane_chunk(half)
                s = raw[h][:, hl] * SCALE2 + al_ref[h, :, hl] + rowc[:, lanes]
                m_old = m_ref[:, lanes]
                m_new = jnp.maximum(m_old, jnp.max(s, axis=0, keepdims=True))
                a = jnp.exp2(m_old - m_new)
                p = jnp.exp2(s - m_new)
                l_ref[:, lanes] = a * l_ref[:, lanes] + jnp.sum(p, axis=0, keepdims=True)
                m_ref[:, lanes] = m_new
                ps.append(p.astype(BF16))
                sc.append(a)
            probs.append(jnp.concatenate(ps, axis=1))
            scales.append(jnp.concatenate(sc, axis=1))
        for h, pv in zip(heads, values_of(i, probs)):
            acc_ref[:, head_lanes(h)] = scales[h] * acc_ref[:, head_lanes(h)] + pv
        return carry

    lax.fori_loop(0, j, past_block, 0)
    for h in heads:
        ot = acc_ref[:, head_lanes(h)] / l_ref[:, head_lanes(h)]
        o_ref[0, :, _head(h)] = ot.T.astype(o_ref.dtype)
    _memory_heads(qm_ref, mk_ref, mvt_ref, o_ref)


def _moba_layer(proj, k4, vt4, kmeans, mkv, mvt):
    b, t, _ = proj.shape
    mlen = mkv.shape[1]
    tq = MOBA_BLOCK
    nb = t // tq
    hq = N_Q_HEADS * tq
    return pl.pallas_call(
        _moba_kernel,
        grid=(b, nb),
        in_specs=[
            pl.BlockSpec((1, tq, Q_W), lambda bi, n: (bi, n, 0)),
            pl.BlockSpec((1, nb, tq, KV_W), lambda bi, n: (bi, 0, 0, 0)),
            pl.BlockSpec((1, nb, KV_W, tq), lambda bi, n: (bi, 0, 0, 0)),
            pl.BlockSpec((1, nb, KV_W), lambda bi, n: (bi, 0, 0)),
            pl.BlockSpec((1, tq, MQ_W), lambda bi, n: (bi, n, Q_W // MQ_W)),
            pl.BlockSpec((1, mlen, MQ_W), lambda bi, n: (bi, 0, 0)),
            pl.BlockSpec((1, MQ_W, mlen), lambda bi, n: (bi, 0, 0)),
        ],
        out_specs=pl.BlockSpec((1, tq, MIX_W), lambda bi, n: (bi, n, 0)),
        out_shape=jax.ShapeDtypeStruct((b, t, MIX_W), BF16),
        scratch_shapes=[pltpu.VMEM((N_Q_HEADS, tq, tq), F32),
                        pltpu.VMEM((nb, 1, hq), F32),
                        pltpu.VMEM((1, hq), F32),
                        pltpu.VMEM((1, hq), F32),
                        pltpu.VMEM((HEAD_DIM, hq), F32)],
        compiler_params=_params(("parallel", "arbitrary")),
        name="moba_mem_attn",
    )(proj, k4, vt4, kmeans, proj, mkv, mvt)


OUT_ROW_CHUNK = 256


def _out_proj_kernel(mix_ref, w_ref, h_ref, g_ref, b_ref, o_ref):
    for r0 in range(0, o_ref.shape[0], OUT_ROW_CHUNK):
        rows = slice(r0, r0 + OUT_ROW_CHUNK)
        r = ALPHA * h_ref[rows, :] + _dot(mix_ref[rows, :], w_ref[...])
        o_ref[rows, :] = _layer_norm(r, g_ref[...], b_ref[...])


def _ln_specs(ln_index):
    spec = pl.BlockSpec((None, 1, D_MODEL), lambda *_: (ln_index, 0, 0))
    return [spec, spec]


def _out_proj_ln(mix, w, layer, h, ln_g, ln_b, tm):
    m, k = mix.shape
    n = w.shape[2]
    return pl.pallas_call(
        _out_proj_kernel,
        grid=(m // tm,),
        in_specs=[pl.BlockSpec((tm, k), lambda i: (i, 0)),
                  pl.BlockSpec((None, k, n), lambda i: (layer, 0, 0), pipeline_mode=pl.Buffered(1)),
                  pl.BlockSpec((tm, n), lambda i: (i, 0))] + _ln_specs(2 * layer),
        out_specs=pl.BlockSpec((tm, n), lambda i: (i, 0)),
        out_shape=jax.ShapeDtypeStruct((m, n), F32),
        compiler_params=_params(("parallel",)),
        name="out_proj_ln",
    )(mix, w, h, ln_g, ln_b)


def _ffn_kernel(h_ref, wu_ref, wd_ref, g_ref, b_ref, o_ref, ob_ref):
    f = pl.program_id(1)

    @pl.when(f == 0)
    def _():
        ob_ref[...] = h_ref[...].astype(BF16)
        o_ref[...] = ALPHA * h_ref[...]

    u = jnp.maximum(_dot(ob_ref[...], wu_ref[...].astype(BF16)), 0.0)
    o_ref[...] += _dot((u * u).astype(BF16), wd_ref[...].astype(BF16))

    @pl.when(f == pl.num_programs(1) - 1)
    def _():
        y = _layer_norm(o_ref[...], g_ref[...], b_ref[...])
        o_ref[...] = y
        ob_ref[...] = y.astype(BF16)


def _ffn_ln(h, w_up, w_down, layer, ln_g, ln_b, tm, tf):
    m, d = h.shape
    dff = w_up.shape[2]
    return pl.pallas_call(
        _ffn_kernel,
        grid=(m // tm, dff // tf),
        in_specs=[pl.BlockSpec((tm, d), lambda i, f: (i, 0), pipeline_mode=pl.Buffered(1)),
                  pl.BlockSpec((None, d, tf), lambda i, f: (layer, 0, f)),
                  pl.BlockSpec((None, tf, d), lambda i, f: (layer, f, 0))] + _ln_specs(2 * layer + 1),
        out_specs=[pl.BlockSpec((tm, d), lambda i, f: (i, 0)),
                   pl.BlockSpec((tm, d), lambda i, f: (i, 0))],
        out_shape=[jax.ShapeDtypeStruct((m, d), F32), jax.ShapeDtypeStruct((m, d), BF16)],
        compiler_params=_params(("parallel", "arbitrary")),
        name="ffn_ln",
    )(h, w_up, w_down, ln_g, ln_b)


PROJ_TM_F32, PROJ_TM_BF16, PROJ_TN = 1024, 2048, 512
KV_TM = 1024
OUT_TM = 1024
FFN_TM, FFN_TF = 1024, 512
SWA_TQ = 512


def kernel(x, mem, w_in_a, sinks_a, w_q_b, w_kv_shared, w_mem_kv, w_o, w_up, w_down, ln_g, ln_b):
    b, t, d = x.shape
    mlen = mem.shape[1]
    m = b * t
    nb = t // MOBA_BLOCK
    h = x.reshape(m, d)
    mem2 = mem.reshape(b * mlen, d)
    w_o_bf = w_o.astype(BF16)
    ln_g2 = ln_g.reshape(2 * DEPTH, 1, d)
    ln_b2 = ln_b.reshape(2 * DEPTH, 1, d)
    k4 = vt4 = kmeans = None
    hx = h
    for layer in range(DEPTH):
        proj_tm = PROJ_TM_F32 if hx.dtype == F32 else PROJ_TM_BF16
        mkv = _matmul(mem2, w_mem_kv, layer, b * mlen, 512, "mem_kv")
        mkv = mkv.reshape(b, mlen, 2 * MQ_W)
        mvt = jnp.swapaxes(mkv[:, :, MQ_W:], 1, 2)
        if layer < N_A_LAYERS:
            proj = _matmul(hx, w_in_a, layer, proj_tm, PROJ_TN, "proj_a").reshape(b, t, -1)
            vt = jnp.swapaxes(proj[:, :, Q_W + KV_W:Q_W + 2 * KV_W], 1, 2)
            mix = _swa_layer(proj, vt, mkv, mvt, sinks_a[layer], SWA_TQ)
        else:
            if k4 is None:
                kv, km = _shared_kv(hx, w_kv_shared, KV_TM)
                k4 = kv.reshape(b, nb, MOBA_BLOCK, 2 * KV_W)
                vt4 = jnp.swapaxes(k4[..., KV_W:], 2, 3)
                kmeans = km.reshape(b, nb, KV_W)
            proj = _matmul(hx, w_q_b, layer - N_A_LAYERS, proj_tm, PROJ_TN, "proj_b")
            mix = _moba_layer(proj.reshape(b, t, -1), k4, vt4, kmeans, mkv, mvt)
        h = _out_proj_ln(mix.reshape(m, MIX_W), w_o_bf, layer, h, ln_g2, ln_b2, OUT_TM)
        h, hx = _ffn_ln(h, w_up, w_down, layer, ln_g2, ln_b2, FFN_TM, FFN_TF)
    return h.reshape(b, t, d)
```

```python
import functools
import math

import jax
import jax.numpy as jnp
from jax import lax
from jax.experimental import pallas as pl
from jax.experimental.pallas import tpu as pltpu

D_MODEL = 2048
DEPTH = 4
HEAD_DIM = 128
N_Q_HEADS = 12
N_KV_HEADS = 4
GROUP = N_Q_HEADS // N_KV_HEADS
N_MEM_HEADS = 4
WINDOW = 128
MOBA_BLOCK = 256
MOBA_TOPK = 3
D_FF = 4 * D_MODEL
N_A_LAYERS = DEPTH // 2
ALPHA = (2 * DEPTH) ** 0.25
LN_EPS = 1e-5
NEG_INF = -1e30
Q_W = N_Q_HEADS * HEAD_DIM
KV_W = N_KV_HEADS * HEAD_DIM
MQ_W = N_MEM_HEADS * HEAD_DIM
MIX_W = Q_W + MQ_W
SCALE = HEAD_DIM ** -0.5
SLOPES = tuple(2.0 ** (-8.0 * h / N_Q_HEADS) for h in range(1, N_Q_HEADS + 1))
LOG2E = math.log2(math.e)
SCALE2 = SCALE * LOG2E
SLOPES2 = tuple(s * LOG2E for s in SLOPES)

V7X_VMEM_BYTES = 64 * 1024 * 1024
VMEM_LIMIT = V7X_VMEM_BYTES - 8 * 1024 * 1024
LANES = 128

BF16 = jnp.bfloat16
F32 = jnp.float32


def _params(sem):
    return pltpu.CompilerParams(dimension_semantics=sem, vmem_limit_bytes=VMEM_LIMIT)


def _dot(a, b):
    return jnp.dot(a, b, preferred_element_type=F32)


def _dot_nt(a, b):
    return lax.dot_general(a, b, (((1,), (1,)), ((), ())), preferred_element_type=F32)


def _head(h):
    return slice(h * HEAD_DIM, (h + 1) * HEAD_DIM)


def _lane_chunk(c):
    return slice(c * LANES, (c + 1) * LANES)


def _tile_scale(scaled_tiles):
    if scaled_tiles is None or not any(scaled_tiles):
        return None
    if all(scaled_tiles):
        return SCALE2
    j = pl.program_id(1)
    is_scaled = functools.reduce(jnp.logical_or, [j == t for t, on in enumerate(scaled_tiles) if on])
    return jnp.where(is_scaled, SCALE2, 1.0)


def _mm_kernel_f32(x_ref, w_ref, o_ref, xb_ref, *, scaled_tiles=None):
    @pl.when(pl.program_id(1) == 0)
    def _():
        xb_ref[...] = x_ref[...].astype(BF16)

    acc = _dot(xb_ref[...], w_ref[...].astype(BF16))
    scale = _tile_scale(scaled_tiles)
    o_ref[...] = (acc if scale is None else acc * scale).astype(o_ref.dtype)


def _mm_kernel_bf16(x_ref, w_ref, o_ref, *, scaled_tiles=None):
    acc = _dot(x_ref[...], w_ref[...].astype(BF16))
    scale = _tile_scale(scaled_tiles)
    o_ref[...] = (acc if scale is None else acc * scale).astype(o_ref.dtype)


def _matmul(x, w, layer, tm, tn, name, query_cols=()):
    m, k = x.shape
    n = w.shape[2]
    is_f32 = x.dtype == F32
    scaled_tiles = tuple(any(lo <= j * tn and (j + 1) * tn <= hi for lo, hi in query_cols)
                         for j in range(n // tn))
    assert sum(scaled_tiles) * tn == sum(hi - lo for lo, hi in query_cols)
    body = functools.partial(_mm_kernel_f32 if is_f32 else _mm_kernel_bf16, scaled_tiles=scaled_tiles)
    return pl.pallas_call(
        body,
        grid=(m // tm, n // tn),
        in_specs=[pl.BlockSpec((tm, k), lambda i, j: (i, 0)),
                  pl.BlockSpec((None, k, tn), lambda i, j: (layer, 0, j))],
        out_specs=pl.BlockSpec((tm, tn), lambda i, j: (i, j)),
        out_shape=jax.ShapeDtypeStruct((m, n), BF16),
        scratch_shapes=[pltpu.VMEM((tm, k), BF16)] if is_f32 else [],
        compiler_params=_params(("parallel", "arbitrary")),
        name=name,
    )(x, w)


def _matmul_all_layers(x, w, tn, name):
    m, k = x.shape
    nl, _, n = w.shape
    return pl.pallas_call(
        _mm_kernel_f32,
        grid=(nl, n // tn),
        in_specs=[pl.BlockSpec((m, k), lambda l, j: (0, 0)),
                  pl.BlockSpec((None, k, tn), lambda l, j: (l, 0, j))],
        out_specs=pl.BlockSpec((None, m, tn), lambda l, j: (l, 0, j)),
        out_shape=jax.ShapeDtypeStruct((nl, m, n), BF16),
        scratch_shapes=[pltpu.VMEM((m, k), BF16)],
        compiler_params=_params(("arbitrary", "arbitrary")),
        name=name,
    )(x, w)


def _kv_kernel(x_ref, w_ref, kv_ref, km_ref):
    acc = _dot(x_ref[...].astype(BF16), w_ref[...].astype(BF16))
    kv_ref[...] = acc.astype(kv_ref.dtype)
    nblk = acc.shape[0] // MOBA_BLOCK
    k = acc[:, :KV_W].reshape(nblk, MOBA_BLOCK, KV_W)
    km_ref[...] = jnp.mean(k, axis=1)[:, None, :]


def _shared_kv(x, w, tm):
    m, k = x.shape
    n = w.shape[1]
    nblk = tm // MOBA_BLOCK
    return pl.pallas_call(
        _kv_kernel,
        grid=(m // tm,),
        in_specs=[pl.BlockSpec((tm, k), lambda i: (i, 0)),
                  pl.BlockSpec((k, n), lambda i: (0, 0))],
        out_specs=[pl.BlockSpec((tm, n), lambda i: (i, 0)),
                   pl.BlockSpec((nblk, 1, KV_W), lambda i: (i, 0, 0))],
        out_shape=[jax.ShapeDtypeStruct((m, n), BF16),
                   jax.ShapeDtypeStruct((m // MOBA_BLOCK, 1, KV_W), F32)],
        compiler_params=_params(("parallel",)),
        name="shared_kv",
    )(x, w)


def _layer_norm(r, g, b):
    mu = jnp.mean(r, axis=-1, keepdims=True)
    c = r - mu
    var = jnp.mean(c * c, axis=-1, keepdims=True)
    return c * lax.rsqrt(var + LN_EPS) * g + b


def _memory_heads(qm_ref, mk_ref, mvt_ref, o_ref):
    tq = qm_ref.shape[1]
    step = min(tq, 2 * LANES)
    units = [(h, r0) for r0 in range(0, tq, step) for h in range(N_MEM_HEADS)]
    scores = [_dot_nt(mk_ref[0, :, _head(h)], qm_ref[0, r0:r0 + step, _head(h)])
              for h, r0 in units]
    probs, dens = [], []
    for s in scores:
        m = jnp.max(s, axis=0, keepdims=True)
        p = jnp.exp2(s - m)
        dens.append(jnp.sum(p, axis=0, keepdims=True))
        probs.append(p.astype(BF16))
    outs = [_dot(mvt_ref[0, _head(h), :], p) for (h, _), p in zip(units, probs)]
    for (h, r0), ot, l in zip(units, outs, dens):
        o_ref[0, r0:r0 + step, Q_W + h * HEAD_DIM:Q_W + (h + 1) * HEAD_DIM] = (ot / l).T.astype(o_ref.dtype)


def _swa_kernel(sink_ref, q_ref, kp_ref, kc_ref, vtp_ref, vtc_ref, qm_ref, mk_ref, mvt_ref, o_ref):
    tq = q_ref.shape[1]
    n = pl.program_id(1)
    sw = WINDOW
    c_io = lax.broadcasted_iota(jnp.int32, (2 * sw, sw), 0)
    r_io = lax.broadcasted_iota(jnp.int32, (2 * sw, sw), 1)
    dist_i = r_io + sw - c_io
    band = (dist_i >= 0) & (dist_i < sw)
    neg_dist = jnp.where(band, -dist_i.astype(F32), NEG_INF)
    first_row = jnp.where(n > 0, 0, sw)
    neg_dist_first = jnp.where(c_io >= first_row, neg_dist, NEG_INF)
    for sb in range(tq // sw):
        rows = slice(sb * sw, (sb + 1) * sw)
        nd = neg_dist_first if sb == 0 else neg_dist
        scores, vtws = [], []
        for kvh in range(N_KV_HEADS):
            cs = _head(kvh)
            if sb == 0:
                kw = jnp.concatenate([kp_ref[0, :, cs], kc_ref[0, :sw, cs]], axis=0)
                vtw = jnp.concatenate([vtp_ref[0, cs, :], vtc_ref[0, cs, :sw]], axis=1)
            else:
                kw = kc_ref[0, (sb - 1) * sw:(sb + 1) * sw, cs]
                vtw = vtc_ref[0, cs, (sb - 1) * sw:(sb + 1) * sw]
            q = jnp.concatenate([q_ref[0, rows, _head(kvh * GROUP + g)] for g in range(GROUP)], axis=0)
            scores.append(_dot_nt(kw, q))
            vtws.append(vtw)
        probs, denoms = [], []
        for h in range(N_Q_HEADS):
            kvh, g = divmod(h, GROUP)
            s = scores[kvh][:, g * sw:(g + 1) * sw] + SLOPES2[h] * nd
            sink = sink_ref[h] * LOG2E
            m = jnp.maximum(jnp.max(s, axis=0, keepdims=True), sink)
            p = jnp.exp2(s - m)
            denoms.append(jnp.sum(p, axis=0, keepdims=True) + jnp.exp2(sink - m))
            probs.append(p.astype(BF16))
        outs = [_dot(vtws[kvh], jnp.concatenate(probs[kvh * GROUP:(kvh + 1) * GROUP], axis=1))
                for kvh in range(N_KV_HEADS)]
        for h in range(N_Q_HEADS):
            kvh, g = divmod(h, GROUP)
            ot = outs[kvh][:, g * sw:(g + 1) * sw] / denoms[h]
            o_ref[0, rows, _head(h)] = ot.T.astype(o_ref.dtype)
    _memory_heads(qm_ref, mk_ref, mvt_ref, o_ref)


def _swa_layer(proj, vt, mkv, mvt, sinks, tq):
    b, t, _ = proj.shape
    mlen = mkv.shape[1]
    wpb = tq // WINDOW
    kcol = Q_W // KV_W
    prev = lambda n: jnp.maximum(n * wpb - 1, 0)
    return pl.pallas_call(
        _swa_kernel,
        grid=(b, t // tq),
        in_specs=[
            pl.BlockSpec(memory_space=pltpu.SMEM),
            pl.BlockSpec((1, tq, Q_W), lambda bi, n: (bi, n, 0)),
            pl.BlockSpec((1, WINDOW, KV_W), lambda bi, n: (bi, prev(n), kcol)),
            pl.BlockSpec((1, tq, KV_W), lambda bi, n: (bi, n, kcol)),
            pl.BlockSpec((1, KV_W, WINDOW), lambda bi, n: (bi, 0, prev(n))),
            pl.BlockSpec((1, KV_W, tq), lambda bi, n: (bi, 0, n)),
            pl.BlockSpec((1, tq, MQ_W), lambda bi, n: (bi, n, kcol + 2)),
            pl.BlockSpec((1, mlen, MQ_W), lambda bi, n: (bi, 0, 0)),
            pl.BlockSpec((1, MQ_W, mlen), lambda bi, n: (bi, 0, 0)),
        ],
        out_specs=pl.BlockSpec((1, tq, MIX_W), lambda bi, n: (bi, n, 0)),
        out_shape=jax.ShapeDtypeStruct((b, t, MIX_W), BF16),
        compiler_params=_params(("parallel", "arbitrary")),
        name="swa_mem_attn",
    )(sinks, proj, proj, proj, vt, vt, proj, mkv, mvt)


def _moba_kernel(q_ref, k_ref, vt_ref, km_ref, qm_ref, mk_ref, mvt_ref, o_ref,
                 al_ref, sel_ref, m_ref, l_ref, acc_ref):
    tq = MOBA_BLOCK
    nb = k_ref.shape[1]
    per_head = tq // LANES
    j = pl.program_id(1)
    key = lax.broadcasted_iota(jnp.int32, (tq, tq), 0)
    qry = lax.broadcasted_iota(jnp.int32, (tq, tq), 1)
    s_minus_t = (key - qry).astype(F32)
    causal_add = jnp.where(key <= qry, 0.0, NEG_INF)
    blk = lax.broadcasted_iota(jnp.int32, (nb, tq), 0)
    heads = range(N_Q_HEADS)

    @pl.when((pl.program_id(0) == 0) & (j == 0))
    def _():
        for h in heads:
            al_ref[h] = SLOPES2[h] * s_minus_t

    def head_lanes(h):
        return slice(h * tq, (h + 1) * tq)

    def scores_of(i):
        return [_dot_nt(k_ref[0, i, :, _head(h // GROUP)], q_ref[0, :, _head(h)]) for h in heads]

    def values_of(i, probs):
        return [_dot(vt_ref[0, i, _head(h // GROUP), :], probs[h]) for h in heads]

    for h in heads:
        kmb = km_ref[0, :, _head(h // GROUP)].astype(BF16)
        gate = jnp.where(blk < j, _dot_nt(kmb, q_ref[0, :, _head(h)]), NEG_INF)
        cnt = jnp.zeros((nb, tq), F32)
        for ip in range(nb):
            gi = gate[ip:ip + 1, :]
            beats = (gi > gate) | ((gi == gate) & (ip < blk))
            cnt = cnt + jnp.where(beats, 1.0, 0.0)
        sel = jnp.where((cnt < MOBA_TOPK) & (blk < j), 0.0, NEG_INF)
        for ip in range(nb):
            sel_ref[ip, :, head_lanes(h)] = sel[ip:ip + 1, :]
    slope_row = jnp.concatenate([jnp.full((1, tq), SLOPES2[h], F32) for h in heads], axis=1)

    raw = scores_of(j)
    probs = []
    for h in heads:
        ps = []
        for half in range(per_head):
            lanes, hl = _lane_chunk(h * per_head + half), _lane_chunk(half)
            s = raw[h][:, hl] + (al_ref[h, :, hl] + causal_add[:, hl])
            m0 = jnp.max(s, axis=0, keepdims=True)
            p = jnp.exp2(s - m0)
            m_ref[:, lanes] = m0
            l_ref[:, lanes] = jnp.sum(p, axis=0, keepdims=True)
            ps.append(p.astype(BF16))
        probs.append(jnp.concatenate(ps, axis=1))
    for h, pv in zip(heads, values_of(j, probs)):
        acc_ref[:, head_lanes(h)] = pv

    def past_block(i, carry):
        gap = ((j - i) * tq).astype(F32)
        rowc = sel_ref[i] - slope_row * gap
        raw = scores_of(i)
        probs, scales = [], []
        for h in heads:
            ps, sc = [], []
            for half in range(per_head):
                lanes, hl = _lane_chunk(h * per_head + half), _lane_chunk(half)
                s = raw[h][:, hl] + al_ref[h, :, hl]
                m_old = m_ref[:, lanes]
                m_new = jnp.maximum(m_old, jnp.max(s, axis=0, keepdims=True) + rowc[:, lanes])
                a = jnp.exp2(m_old - m_new)
                p = jnp.exp2(s - (m_new - rowc[:, lanes]))
                l_ref[:, lanes] = a * l_ref[:, lanes] + jnp.sum(p, axis=0, keepdims=True)
                m_ref[:, lanes] = m_new
                ps.append(p.astype(BF16))
                sc.append(a)
            probs.append(jnp.concatenate(ps, axis=1))
            scales.append(jnp.concatenate(sc, axis=1))
        for h, pv in zip(heads, values_of(i, probs)):
            acc_ref[:, head_lanes(h)] = scales[h] * acc_ref[:, head_lanes(h)] + pv
        return carry

    lax.fori_loop(0, j, past_block, 0)
    for h in heads:
        ot = acc_ref[:, head_lanes(h)] / l_ref[:, head_lanes(h)]
        o_ref[0, :, _head(h)] = ot.T.astype(o_ref.dtype)
    _memory_heads(qm_ref, mk_ref, mvt_ref, o_ref)


def _moba_layer(proj, k4, vt4, kmeans, mkv, mvt):
    b, t, _ = proj.shape
    mlen = mkv.shape[1]
    tq = MOBA_BLOCK
    nb = t // tq
    hq = N_Q_HEADS * tq
    return pl.pallas_call(
        _moba_kernel,
        grid=(b, nb),
        in_specs=[
            pl.BlockSpec((1, tq, Q_W), lambda bi, n: (bi, n, 0)),
            pl.BlockSpec((1, nb, tq, KV_W), lambda bi, n: (bi, 0, 0, 0)),
            pl.BlockSpec((1, nb, KV_W, tq), lambda bi, n: (bi, 0, 0, 0)),
            pl.BlockSpec((1, nb, KV_W), lambda bi, n: (bi, 0, 0)),
            pl.BlockSpec((1, tq, MQ_W), lambda bi, n: (bi, n, Q_W // MQ_W)),
            pl.BlockSpec((1, mlen, MQ_W), lambda bi, n: (bi, 0, 0)),
            pl.BlockSpec((1, MQ_W, mlen), lambda bi, n: (bi, 0, 0)),
        ],
        out_specs=pl.BlockSpec((1, tq, MIX_W), lambda bi, n: (bi, n, 0)),
        out_shape=jax.ShapeDtypeStruct((b, t, MIX_W), BF16),
        scratch_shapes=[pltpu.VMEM((N_Q_HEADS, tq, tq), F32),
                        pltpu.VMEM((nb, 1, hq), F32),
                        pltpu.VMEM((1, hq), F32),
                        pltpu.VMEM((1, hq), F32),
                        pltpu.VMEM((HEAD_DIM, hq), F32)],
        compiler_params=_params(("arbitrary", "arbitrary")),
        name="moba_mem_attn",
    )(proj, k4, vt4, kmeans, proj, mkv, mvt)


OUT_ROW_CHUNK = 256


def _out_proj_kernel(mix_ref, w_ref, h_ref, g_ref, b_ref, o_ref):
    for r0 in range(0, o_ref.shape[0], OUT_ROW_CHUNK):
        rows = slice(r0, r0 + OUT_ROW_CHUNK)
        r = ALPHA * h_ref[rows, :] + _dot(mix_ref[rows, :], w_ref[...])
        o_ref[rows, :] = _layer_norm(r, g_ref[...], b_ref[...])


def _ln_specs(ln_index):
    spec = pl.BlockSpec((None, 1, D_MODEL), lambda *_: (ln_index, 0, 0))
    return [spec, spec]


def _out_proj_ln(mix, w, layer, h, ln_g, ln_b, tm):
    m, k = mix.shape
    n = w.shape[2]
    return pl.pallas_call(
        _out_proj_kernel,
        grid=(m // tm,),
        in_specs=[pl.BlockSpec((tm, k), lambda i: (i, 0)),
                  pl.BlockSpec((None, k, n), lambda i: (layer, 0, 0), pipeline_mode=pl.Buffered(1)),
                  pl.BlockSpec((tm, n), lambda i: (i, 0))] + _ln_specs(2 * layer),
        out_specs=pl.BlockSpec((tm, n), lambda i: (i, 0)),
        out_shape=jax.ShapeDtypeStruct((m, n), F32),
        compiler_params=_params(("parallel",)),
        name="out_proj_ln",
    )(mix, w, h, ln_g, ln_b)


def _ffn_kernel(h_ref, wu_ref, wd_ref, g_ref, b_ref, o_ref, ob_ref):
    f = pl.program_id(1)

    @pl.when(f == 0)
    def _():
        ob_ref[...] = h_ref[...].astype(BF16)
        o_ref[...] = ALPHA * h_ref[...]

    u = jnp.maximum(_dot(ob_ref[...], wu_ref[...].astype(BF16)), 0.0)
    o_ref[...] += _dot((u * u).astype(BF16), wd_ref[...].astype(BF16))

    @pl.when(f == pl.num_programs(1) - 1)
    def _():
        y = _layer_norm(o_ref[...], g_ref[...], b_ref[...])
        o_ref[...] = y
        ob_ref[...] = y.astype(BF16)


def _ffn_ln(h, w_up, w_down, layer, ln_g, ln_b, tm, tf):
    m, d = h.shape
    dff = w_up.shape[2]
    return pl.pallas_call(
        _ffn_kernel,
        grid=(m // tm, dff // tf),
        in_specs=[pl.BlockSpec((tm, d), lambda i, f: (i, 0), pipeline_mode=pl.Buffered(1)),
                  pl.BlockSpec((None, d, tf), lambda i, f: (layer, 0, f)),
                  pl.BlockSpec((None, tf, d), lambda i, f: (layer, f, 0))] + _ln_specs(2 * layer + 1),
        out_specs=[pl.BlockSpec((tm, d), lambda i, f: (i, 0)),
                   pl.BlockSpec((tm, d), lambda i, f: (i, 0))],
        out_shape=[jax.ShapeDtypeStruct((m, d), F32), jax.ShapeDtypeStruct((m, d), BF16)],
        compiler_params=_params(("parallel", "arbitrary")),
        name="ffn_ln",
    )(h, w_up, w_down, ln_g, ln_b)


PROJ_TM_F32, PROJ_TM_BF16, PROJ_TN = 1024, 2048, 512
KV_TM = 1024
OUT_TM = 1024
FFN_TM, FFN_TF = 1024, 512
SWA_TQ = 512


def kernel(x, mem, w_in_a, sinks_a, w_q_b, w_kv_shared, w_mem_kv, w_o, w_up, w_down, ln_g, ln_b):
    b, t, d = x.shape
    mlen = mem.shape[1]
    m = b * t
    nb = t // MOBA_BLOCK
    h = x.reshape(m, d)
    mem2 = mem.reshape(b * mlen, d)
    w_o_bf = w_o.astype(BF16)
    ln_g2 = ln_g.reshape(2 * DEPTH, 1, d)
    ln_b2 = ln_b.reshape(2 * DEPTH, 1, d)
    mkv_all = _matmul_all_layers(mem2, w_mem_kv, 512, "mem_kv").reshape(DEPTH, b, mlen, 2 * MQ_W)
    mvt_all = jnp.swapaxes(mkv_all[..., MQ_W:], 2, 3)
    k4 = vt4 = kmeans = None
    hx = h
    for layer in range(DEPTH):
        proj_tm = PROJ_TM_F32 if hx.dtype == F32 else PROJ_TM_BF16
        mkv, mvt = mkv_all[layer], mvt_all[layer]
        if layer < N_A_LAYERS:
            proj = _matmul(hx, w_in_a, layer, proj_tm, PROJ_TN, "proj_a",
                           query_cols=((0, Q_W), (Q_W + 2 * KV_W, Q_W + 2 * KV_W + MQ_W))).reshape(b, t, -1)
            vt = jnp.swapaxes(proj[:, :, Q_W + KV_W:Q_W + 2 * KV_W], 1, 2)
            mix = _swa_layer(proj, vt, mkv, mvt, sinks_a[layer], SWA_TQ)
        else:
            if k4 is None:
                kv, km = _shared_kv(hx, w_kv_shared, KV_TM)
                k4 = kv.reshape(b, nb, MOBA_BLOCK, 2 * KV_W)
                vt4 = jnp.swapaxes(k4[..., KV_W:], 2, 3)
                kmeans = km.reshape(b, nb, KV_W)
            proj = _matmul(hx, w_q_b, layer - N_A_LAYERS, proj_tm, PROJ_TN, "proj_b", query_cols=((0, MIX_W),))
            mix = _moba_layer(proj.reshape(b, t, -1), k4, vt4, kmeans, mkv, mvt)
        h = _out_proj_ln(mix.reshape(m, MIX_W), w_o_bf, layer, h, ln_g2, ln_b2, OUT_TM)
        h, hx = _ffn_ln(h, w_up, w_down, layer, ln_g2, ln_b2, FFN_TM, FFN_TF)
    return h.reshape(b, t, d)
```

```python
import functools
import math

import jax
import jax.numpy as jnp
from jax import lax
from jax.experimental import pallas as pl
from jax.experimental.pallas import tpu as pltpu

D_MODEL = 2048
DEPTH = 4
HEAD_DIM = 128
N_Q_HEADS = 12
N_KV_HEADS = 4
GROUP = N_Q_HEADS // N_KV_HEADS
N_MEM_HEADS = 4
WINDOW = 128
MOBA_BLOCK = 256
MOBA_TOPK = 3
D_FF = 4 * D_MODEL
N_A_LAYERS = DEPTH // 2
ALPHA = (2 * DEPTH) ** 0.25
LN_EPS = 1e-5
NEG_INF = -1e30
Q_W = N_Q_HEADS * HEAD_DIM
KV_W = N_KV_HEADS * HEAD_DIM
MQ_W = N_MEM_HEADS * HEAD_DIM
MIX_W = Q_W + MQ_W
SCALE = HEAD_DIM ** -0.5
SLOPES = tuple(2.0 ** (-8.0 * h / N_Q_HEADS) for h in range(1, N_Q_HEADS + 1))
LOG2E = math.log2(math.e)
SCALE2 = SCALE * LOG2E
SLOPES2 = tuple(s * LOG2E for s in SLOPES)

V7X_VMEM_BYTES = 64 * 1024 * 1024
VMEM_LIMIT = V7X_VMEM_BYTES - 8 * 1024 * 1024
LANES = 128

BF16 = jnp.bfloat16
F32 = jnp.float32


def _params(sem):
    return pltpu.CompilerParams(dimension_semantics=sem, vmem_limit_bytes=VMEM_LIMIT)


def _dot(a, b):
    return jnp.dot(a, b, preferred_element_type=F32)


def _dot_nt(a, b):
    return lax.dot_general(a, b, (((1,), (1,)), ((), ())), preferred_element_type=F32)


def _head(h):
    return slice(h * HEAD_DIM, (h + 1) * HEAD_DIM)


def _lane_chunk(c):
    return slice(c * LANES, (c + 1) * LANES)


def _tile_scale(scaled_tiles):
    if scaled_tiles is None or not any(scaled_tiles):
        return None
    if all(scaled_tiles):
        return SCALE2
    j = pl.program_id(1)
    is_scaled = functools.reduce(jnp.logical_or, [j == t for t, on in enumerate(scaled_tiles) if on])
    return jnp.where(is_scaled, SCALE2, 1.0)


def _mm_kernel_f32(x_ref, w_ref, o_ref, xb_ref, *, scaled_tiles=None):
    @pl.when(pl.program_id(1) == 0)
    def _():
        xb_ref[...] = x_ref[...].astype(BF16)

    acc = _dot(xb_ref[...], w_ref[...].astype(BF16))
    scale = _tile_scale(scaled_tiles)
    o_ref[...] = (acc if scale is None else acc * scale).astype(o_ref.dtype)


def _mm_kernel_bf16(x_ref, w_ref, o_ref, *, scaled_tiles=None):
    acc = _dot(x_ref[...], w_ref[...].astype(BF16))
    scale = _tile_scale(scaled_tiles)
    o_ref[...] = (acc if scale is None else acc * scale).astype(o_ref.dtype)


def _matmul(x, w, layer, tm, tn, name, query_cols=()):
    m, k = x.shape
    n = w.shape[2]
    is_f32 = x.dtype == F32
    scaled_tiles = tuple(any(lo <= j * tn and (j + 1) * tn <= hi for lo, hi in query_cols)
                         for j in range(n // tn))
    assert sum(scaled_tiles) * tn == sum(hi - lo for lo, hi in query_cols)
    body = functools.partial(_mm_kernel_f32 if is_f32 else _mm_kernel_bf16, scaled_tiles=scaled_tiles)
    return pl.pallas_call(
        body,
        grid=(m // tm, n // tn),
        in_specs=[pl.BlockSpec((tm, k), lambda i, j: (i, 0)),
                  pl.BlockSpec((None, k, tn), lambda i, j: (layer, 0, j))],
        out_specs=pl.BlockSpec((tm, tn), lambda i, j: (i, j)),
        out_shape=jax.ShapeDtypeStruct((m, n), BF16),
        scratch_shapes=[pltpu.VMEM((tm, k), BF16)] if is_f32 else [],
        compiler_params=_params(("parallel", "arbitrary")),
        name=name,
    )(x, w)


def _matmul_all_layers(x, w, tn, name):
    m, k = x.shape
    nl, _, n = w.shape
    return pl.pallas_call(
        _mm_kernel_f32,
        grid=(nl, n // tn),
        in_specs=[pl.BlockSpec((m, k), lambda l, j: (0, 0)),
                  pl.BlockSpec((None, k, tn), lambda l, j: (l, 0, j))],
        out_specs=pl.BlockSpec((None, m, tn), lambda l, j: (l, 0, j)),
        out_shape=jax.ShapeDtypeStruct((nl, m, n), BF16),
        scratch_shapes=[pltpu.VMEM((m, k), BF16)],
        compiler_params=_params(("arbitrary", "arbitrary")),
        name=name,
    )(x, w)


def _kv_kernel(x_ref, w_ref, kv_ref, km_ref):
    acc = _dot(x_ref[...].astype(BF16), w_ref[...].astype(BF16))
    kv_ref[...] = acc.astype(kv_ref.dtype)
    nblk = acc.shape[0] // MOBA_BLOCK
    k = acc[:, :KV_W].reshape(nblk, MOBA_BLOCK, KV_W)
    km_ref[...] = jnp.mean(k, axis=1)[:, None, :]


def _shared_kv(x, w, tm):
    m, k = x.shape
    n = w.shape[1]
    nblk = tm // MOBA_BLOCK
    return pl.pallas_call(
        _kv_kernel,
        grid=(m // tm,),
        in_specs=[pl.BlockSpec((tm, k), lambda i: (i, 0)),
                  pl.BlockSpec((k, n), lambda i: (0, 0))],
        out_specs=[pl.BlockSpec((tm, n), lambda i: (i, 0)),
                   pl.BlockSpec((nblk, 1, KV_W), lambda i: (i, 0, 0))],
        out_shape=[jax.ShapeDtypeStruct((m, n), BF16),
                   jax.ShapeDtypeStruct((m // MOBA_BLOCK, 1, KV_W), F32)],
        compiler_params=_params(("parallel",)),
        name="shared_kv",
    )(x, w)


def _layer_norm(r, g, b):
    mu = jnp.mean(r, axis=-1, keepdims=True)
    c = r - mu
    var = jnp.mean(c * c, axis=-1, keepdims=True)
    return c * lax.rsqrt(var + LN_EPS) * g + b


def _memory_heads(qm_ref, mk_ref, mvt_ref, o_ref):
    tq = qm_ref.shape[1]
    step = min(tq, 2 * LANES)
    units = [(h, r0) for r0 in range(0, tq, step) for h in range(N_MEM_HEADS)]
    scores = [_dot_nt(mk_ref[0, :, _head(h)], qm_ref[0, r0:r0 + step, _head(h)])
              for h, r0 in units]
    probs, dens = [], []
    for s in scores:
        m = jnp.max(s, axis=0, keepdims=True)
        p = jnp.exp2(s - m)
        dens.append(jnp.sum(p, axis=0, keepdims=True))
        probs.append(p.astype(BF16))
    outs = [_dot(mvt_ref[0, _head(h), :], p) for (h, _), p in zip(units, probs)]
    for (h, r0), ot, l in zip(units, outs, dens):
        o_ref[0, r0:r0 + step, Q_W + h * HEAD_DIM:Q_W + (h + 1) * HEAD_DIM] = (ot / l).T.astype(o_ref.dtype)


def _swa_kernel(sink_ref, q_ref, kp_ref, kc_ref, vtp_ref, vtc_ref, qm_ref, mk_ref, mvt_ref, o_ref):
    tq = q_ref.shape[1]
    n = pl.program_id(1)
    sw = WINDOW
    c_io = lax.broadcasted_iota(jnp.int32, (2 * sw, sw), 0)
    r_io = lax.broadcasted_iota(jnp.int32, (2 * sw, sw), 1)
    dist_i = r_io + sw - c_io
    band = (dist_i >= 0) & (dist_i < sw)
    neg_dist = jnp.where(band, -dist_i.astype(F32), NEG_INF)
    first_row = jnp.where(n > 0, 0, sw)
    neg_dist_first = jnp.where(c_io >= first_row, neg_dist, NEG_INF)
    for sb in range(tq // sw):
        rows = slice(sb * sw, (sb + 1) * sw)
        nd = neg_dist_first if sb == 0 else neg_dist
        scores, vtws = [], []
        for kvh in range(N_KV_HEADS):
            cs = _head(kvh)
            if sb == 0:
                kw = jnp.concatenate([kp_ref[0, :, cs], kc_ref[0, :sw, cs]], axis=0)
                vtw = jnp.concatenate([vtp_ref[0, cs, :], vtc_ref[0, cs, :sw]], axis=1)
            else:
                kw = kc_ref[0, (sb - 1) * sw:(sb + 1) * sw, cs]
                vtw = vtc_ref[0, cs, (sb - 1) * sw:(sb + 1) * sw]
            q = jnp.concatenate([q_ref[0, rows, _head(kvh * GROUP + g)] for g in range(GROUP)], axis=0)
            scores.append(_dot_nt(kw, q))
            vtws.append(vtw)
        probs, denoms = [], []
        for h in range(N_Q_HEADS):
            kvh, g = divmod(h, GROUP)
            s = scores[kvh][:, g * sw:(g + 1) * sw] + SLOPES2[h] * nd
            sink = sink_ref[h] * LOG2E
            m = jnp.maximum(jnp.max(s, axis=0, keepdims=True), sink)
            p = jnp.exp2(s - m)
            denoms.append(jnp.sum(p, axis=0, keepdims=True) + jnp.exp2(sink - m))
            probs.append(p.astype(BF16))
        outs = [_dot(vtws[kvh], jnp.concatenate(probs[kvh * GROUP:(kvh + 1) * GROUP], axis=1))
                for kvh in range(N_KV_HEADS)]
        for h in range(N_Q_HEADS):
            kvh, g = divmod(h, GROUP)
            ot = outs[kvh][:, g * sw:(g + 1) * sw] / denoms[h]
            o_ref[0, rows, _head(h)] = ot.T.astype(o_ref.dtype)
    _memory_heads(qm_ref, mk_ref, mvt_ref, o_ref)


def _swa_layer(proj, vt, mkv, mvt, sinks, tq):
    b, t, _ = proj.shape
    mlen = mkv.shape[1]
    wpb = tq // WINDOW
    kcol = Q_W // KV_W
    prev = lambda n: jnp.maximum(n * wpb - 1, 0)
    return pl.pallas_call(
        _swa_kernel,
        grid=(b, t // tq),
        in_specs=[
            pl.BlockSpec(memory_space=pltpu.SMEM),
            pl.BlockSpec((1, tq, Q_W), lambda bi, n: (bi, n, 0)),
            pl.BlockSpec((1, WINDOW, KV_W), lambda bi, n: (bi, prev(n), kcol)),
            pl.BlockSpec((1, tq, KV_W), lambda bi, n: (bi, n, kcol)),
            pl.BlockSpec((1, KV_W, WINDOW), lambda bi, n: (bi, 0, prev(n))),
            pl.BlockSpec((1, KV_W, tq), lambda bi, n: (bi, 0, n)),
            pl.BlockSpec((1, tq, MQ_W), lambda bi, n: (bi, n, kcol + 2)),
            pl.BlockSpec((1, mlen, MQ_W), lambda bi, n: (bi, 0, 0)),
            pl.BlockSpec((1, MQ_W, mlen), lambda bi, n: (bi, 0, 0)),
        ],
        out_specs=pl.BlockSpec((1, tq, MIX_W), lambda bi, n: (bi, n, 0)),
        out_shape=jax.ShapeDtypeStruct((b, t, MIX_W), BF16),
        compiler_params=_params(("parallel", "arbitrary")),
        name="swa_mem_attn",
    )(sinks, proj, proj, proj, vt, vt, proj, mkv, mvt)


def _moba_kernel(q_ref, k_ref, vt_ref, km_ref, qm_ref, mk_ref, mvt_ref, o_ref,
                 al_ref, sel_ref, m_ref, l_ref, acc_ref):
    tq = MOBA_BLOCK
    nb = k_ref.shape[1]
    per_head = tq // LANES
    j = pl.program_id(1)
    key = lax.broadcasted_iota(jnp.int32, (tq, tq), 0)
    qry = lax.broadcasted_iota(jnp.int32, (tq, tq), 1)
    s_minus_t = (key - qry).astype(F32)
    causal_add = jnp.where(key <= qry, 0.0, NEG_INF)
    blk = lax.broadcasted_iota(jnp.int32, (nb, tq), 0)
    heads = range(N_Q_HEADS)

    @pl.when((pl.program_id(0) == 0) & (j == 0))
    def _():
        for h in heads:
            al_ref[h] = SLOPES2[h] * s_minus_t

    def head_lanes(h):
        return slice(h * tq, (h + 1) * tq)

    def scores_of(i):
        return [_dot_nt(k_ref[0, i, :, _head(h // GROUP)], q_ref[0, :, _head(h)]) for h in heads]

    def values_of(i, probs):
        return [_dot(vt_ref[0, i, _head(h // GROUP), :], probs[h]) for h in heads]

    for h in heads:
        kmb = km_ref[0, :, _head(h // GROUP)].astype(BF16)
        gate = jnp.where(blk < j, _dot_nt(kmb, q_ref[0, :, _head(h)]), NEG_INF)
        cnt = jnp.zeros((nb, tq), F32)
        for ip in range(nb):
            gi = gate[ip:ip + 1, :]
            beats = (gi > gate) | ((gi == gate) & (ip < blk))
            cnt = cnt + jnp.where(beats, 1.0, 0.0)
        sel = jnp.where((cnt < MOBA_TOPK) & (blk < j), 0.0, NEG_INF)
        for ip in range(nb):
            sel_ref[ip, :, head_lanes(h)] = sel[ip:ip + 1, :]
    slope_row = jnp.concatenate([jnp.full((1, tq), SLOPES2[h], F32) for h in heads], axis=1)

    raw = scores_of(j)
    probs = []
    for h in heads:
        ps = []
        for half in range(per_head):
            lanes, hl = _lane_chunk(h * per_head + half), _lane_chunk(half)
            s = raw[h][:, hl] + (al_ref[h, :, hl] + causal_add[:, hl])
            m0 = jnp.max(s, axis=0, keepdims=True)
            p = jnp.exp2(s - m0)
            m_ref[:, lanes] = m0
            l_ref[:, lanes] = jnp.sum(p, axis=0, keepdims=True)
            ps.append(p.astype(BF16))
        probs.append(jnp.concatenate(ps, axis=1))
    for h, pv in zip(heads, values_of(j, probs)):
        acc_ref[:, head_lanes(h)] = pv

    def past_block(i, carry):
        gap = ((j - i) * tq).astype(F32)
        rowc = sel_ref[i] - slope_row * gap
        raw = scores_of(i)
        probs, scales = [], []
        for h in heads:
            ps, sc = [], []
            for half in range(per_head):
                lanes, hl = _lane_chunk(h * per_head + half), _lane_chunk(half)
                s = raw[h][:, hl] + al_ref[h, :, hl]
                m_old = m_ref[:, lanes]
                m_new = jnp.maximum(m_old, jnp.max(s, axis=0, keepdims=True) + rowc[:, lanes])
                a = jnp.exp2(m_old - m_new)
                p = jnp.exp2(s - (m_new - rowc[:, lanes]))
                l_ref[:, lanes] = a * l_ref[:, lanes] + jnp.sum(p, axis=0, keepdims=True)
                m_ref[:, lanes] = m_new
                ps.append(p.astype(BF16))
                sc.append(a)
            probs.append(jnp.concatenate(ps, axis=1))
            scales.append(jnp.concatenate(sc, axis=1))
        for h, pv in zip(heads, values_of(i, probs)):
            acc_ref[:, head_lanes(h)] = scales[h] * acc_ref[:, head_lanes(h)] + pv
        return carry

    lax.fori_loop(0, j, past_block, 0)
    for h in heads:
        ot = acc_ref[:, head_lanes(h)] / l_ref[:, head_lanes(h)]
        o_ref[0, :, _head(h)] = ot.T.astype(o_ref.dtype)
    _memory_heads(qm_ref, mk_ref, mvt_ref, o_ref)


def _moba_layer(proj, k4, vt4, kmeans, mkv, mvt):
    b, t, _ = proj.shape
    mlen = mkv.shape[1]
    tq = MOBA_BLOCK
    nb = t // tq
    hq = N_Q_HEADS * tq
    return pl.pallas_call(
        _moba_kernel,
        grid=(b, nb),
        in_specs=[
            pl.BlockSpec((1, tq, Q_W), lambda bi, n: (bi, n, 0)),
            pl.BlockSpec((1, nb, tq, KV_W), lambda bi, n: (bi, 0, 0, 0)),
            pl.BlockSpec((1, nb, KV_W, tq), lambda bi, n: (bi, 0, 0, 0)),
            pl.BlockSpec((1, nb, KV_W), lambda bi, n: (bi, 0, 0)),
            pl.BlockSpec((1, tq, MQ_W), lambda bi, n: (bi, n, Q_W // MQ_W)),
            pl.BlockSpec((1, mlen, MQ_W), lambda bi, n: (bi, 0, 0)),
            pl.BlockSpec((1, MQ_W, mlen), lambda bi, n: (bi, 0, 0)),
        ],
        out_specs=pl.BlockSpec((1, tq, MIX_W), lambda bi, n: (bi, n, 0)),
        out_shape=jax.ShapeDtypeStruct((b, t, MIX_W), BF16),
        scratch_shapes=[pltpu.VMEM((N_Q_HEADS, tq, tq), F32),
                        pltpu.VMEM((nb, 1, hq), F32),
                        pltpu.VMEM((1, hq), F32),
                        pltpu.VMEM((1, hq), F32),
                        pltpu.VMEM((HEAD_DIM, hq), F32)],
        compiler_params=_params(("arbitrary", "arbitrary")),
        name="moba_mem_attn",
    )(proj, k4, vt4, kmeans, proj, mkv, mvt)


OUT_ROW_CHUNK = 256


def _out_proj_kernel(mix_ref, w_ref, h_ref, g_ref, b_ref, o_ref):
    for r0 in range(0, o_ref.shape[0], OUT_ROW_CHUNK):
        rows = slice(r0, r0 + OUT_ROW_CHUNK)
        r = ALPHA * h_ref[rows, :] + _dot(mix_ref[rows, :], w_ref[...])
        o_ref[rows, :] = _layer_norm(r, g_ref[...], b_ref[...])


def _ln_specs(ln_index):
    spec = pl.BlockSpec((None, 1, D_MODEL), lambda *_: (ln_index, 0, 0))
    return [spec, spec]


def _out_proj_ln(mix, w, layer, h, ln_g, ln_b, tm):
    m, k = mix.shape
    n = w.shape[2]
    return pl.pallas_call(
        _out_proj_kernel,
        grid=(m // tm,),
        in_specs=[pl.BlockSpec((tm, k), lambda i: (i, 0)),
                  pl.BlockSpec((None, k, n), lambda i: (layer, 0, 0), pipeline_mode=pl.Buffered(1)),
                  pl.BlockSpec((tm, n), lambda i: (i, 0))] + _ln_specs(2 * layer),
        out_specs=pl.BlockSpec((tm, n), lambda i: (i, 0)),
        out_shape=jax.ShapeDtypeStruct((m, n), F32),
        compiler_params=_params(("parallel",)),
        name="out_proj_ln",
    )(mix, w, h, ln_g, ln_b)


def _ffn_kernel(h_ref, wu_ref, wd_ref, g_ref, b_ref, o_ref, ob_ref):
    f = pl.program_id(1)

    @pl.when(f == 0)
    def _():
        ob_ref[...] = h_ref[...].astype(BF16)
        o_ref[...] = ALPHA * h_ref[...]

    u = jnp.maximum(_dot(ob_ref[...], wu_ref[...].astype(BF16)), 0.0)
    o_ref[...] += _dot((u * u).astype(BF16), wd_ref[...].astype(BF16))

    @pl.when(f == pl.num_programs(1) - 1)
    def _():
        y = _layer_norm(o_ref[...], g_ref[...], b_ref[...])
        o_ref[...] = y
        ob_ref[...] = y.astype(BF16)


def _ffn_ln(h, w_up, w_down, layer, ln_g, ln_b, tm, tf):
    m, d = h.shape
    dff = w_up.shape[2]
    return pl.pallas_call(
        _ffn_kernel,
        grid=(m // tm, dff // tf),
        in_specs=[pl.BlockSpec((tm, d), lambda i, f: (i, 0)),
                  pl.BlockSpec((None, d, tf), lambda i, f: (layer, 0, f)),
                  pl.BlockSpec((None, tf, d), lambda i, f: (layer, f, 0))] + _ln_specs(2 * layer + 1),
        out_specs=[pl.BlockSpec((tm, d), lambda i, f: (i, 0)),
                   pl.BlockSpec((tm, d), lambda i, f: (i, 0))],
        out_shape=[jax.ShapeDtypeStruct((m, d), F32), jax.ShapeDtypeStruct((m, d), BF16)],
        compiler_params=pltpu.CompilerParams(dimension_semantics=("parallel", "arbitrary"),
                                             vmem_limit_bytes=V7X_VMEM_BYTES - 1024 * 1024),
        name="ffn_ln",
    )(h, w_up, w_down, ln_g, ln_b)


PROJ_TM_F32, PROJ_TM_BF16, PROJ_TN = 1024, 2048, 512
KV_TM = 1024
OUT_TM = 1024
FFN_TM, FFN_TF = 1024, 512
SWA_TQ = 512


def kernel(x, mem, w_in_a, sinks_a, w_q_b, w_kv_shared, w_mem_kv, w_o, w_up, w_down, ln_g, ln_b):
    b, t, d = x.shape
    mlen = mem.shape[1]
    m = b * t
    nb = t // MOBA_BLOCK
    h = x.reshape(m, d)
    mem2 = mem.reshape(b * mlen, d)
    w_o_bf = w_o.astype(BF16)
    ln_g2 = ln_g.reshape(2 * DEPTH, 1, d)
    ln_b2 = ln_b.reshape(2 * DEPTH, 1, d)
    mkv_all = _matmul_all_layers(mem2, w_mem_kv, 512, "mem_kv").reshape(DEPTH, b, mlen, 2 * MQ_W)
    mvt_all = jnp.swapaxes(mkv_all[..., MQ_W:], 2, 3)
    k4 = vt4 = kmeans = None
    hx = h
    for layer in range(DEPTH):
        proj_tm = PROJ_TM_F32 if hx.dtype == F32 else PROJ_TM_BF16
        mkv, mvt = mkv_all[layer], mvt_all[layer]
        if layer < N_A_LAYERS:
            proj = _matmul(hx, w_in_a, layer, proj_tm, PROJ_TN, "proj_a",
                           query_cols=((0, Q_W), (Q_W + 2 * KV_W, Q_W + 2 * KV_W + MQ_W))).reshape(b, t, -1)
            vt = jnp.swapaxes(proj[:, :, Q_W + KV_W:Q_W + 2 * KV_W], 1, 2)
            mix = _swa_layer(proj, vt, mkv, mvt, sinks_a[layer], SWA_TQ)
        else:
            if k4 is None:
                kv, km = _shared_kv(hx, w_kv_shared, KV_TM)
                k4 = kv.reshape(b, nb, MOBA_BLOCK, 2 * KV_W)
                vt4 = jnp.swapaxes(k4[..., KV_W:], 2, 3)
                kmeans = km.reshape(b, nb, KV_W)
            proj = _matmul(hx, w_q_b, layer - N_A_LAYERS, proj_tm, PROJ_TN, "proj_b", query_cols=((0, MIX_W),))
            mix = _moba_layer(proj.reshape(b, t, -1), k4, vt4, kmeans, mkv, mvt)
        h = _out_proj_ln(mix.reshape(m, MIX_W), w_o_bf, layer, h, ln_g2, ln_b2, OUT_TM)
        h, hx = _ffn_ln(h, w_up, w_down, layer, ln_g2, ln_b2, FFN_TM, FFN_TF)
    return h.reshape(b, t, d)
```

```python
import functools
import math

import jax
import jax.numpy as jnp
from jax import lax
from jax.experimental import pallas as pl
from jax.experimental.pallas import tpu as pltpu

D_MODEL = 2048
DEPTH = 4
HEAD_DIM = 128
N_Q_HEADS = 12
N_KV_HEADS = 4
GROUP = N_Q_HEADS // N_KV_HEADS
N_MEM_HEADS = 4
WINDOW = 128
MOBA_BLOCK = 256
MOBA_TOPK = 3
D_FF = 4 * D_MODEL
N_A_LAYERS = DEPTH // 2
ALPHA = (2 * DEPTH) ** 0.25
LN_EPS = 1e-5
NEG_INF = -1e30
Q_W = N_Q_HEADS * HEAD_DIM
KV_W = N_KV_HEADS * HEAD_DIM
MQ_W = N_MEM_HEADS * HEAD_DIM
MIX_W = Q_W + MQ_W
SCALE = HEAD_DIM ** -0.5
SLOPES = tuple(2.0 ** (-8.0 * h / N_Q_HEADS) for h in range(1, N_Q_HEADS + 1))
LOG2E = math.log2(math.e)
SCALE2 = SCALE * LOG2E
SLOPES2 = tuple(s * LOG2E for s in SLOPES)

V7X_VMEM_BYTES = 64 * 1024 * 1024
VMEM_LIMIT = V7X_VMEM_BYTES - 8 * 1024 * 1024
LANES = 128

BF16 = jnp.bfloat16
F32 = jnp.float32


VMEM_LIMIT_BIG_TILES = V7X_VMEM_BYTES - 1024 * 1024


def _params(sem, vmem_limit=VMEM_LIMIT):
    return pltpu.CompilerParams(dimension_semantics=sem, vmem_limit_bytes=vmem_limit)


def _dot(a, b):
    return jnp.dot(a, b, preferred_element_type=F32)


def _dot_nt(a, b):
    return lax.dot_general(a, b, (((1,), (1,)), ((), ())), preferred_element_type=F32)


def _head(h):
    return slice(h * HEAD_DIM, (h + 1) * HEAD_DIM)


def _lane_chunk(c):
    return slice(c * LANES, (c + 1) * LANES)


def _tile_scale(scaled_tiles):
    if scaled_tiles is None or not any(scaled_tiles):
        return None
    if all(scaled_tiles):
        return SCALE2
    j = pl.program_id(1)
    is_scaled = functools.reduce(jnp.logical_or, [j == t for t, on in enumerate(scaled_tiles) if on])
    return jnp.where(is_scaled, SCALE2, 1.0)


def _mm_kernel_f32(x_ref, w_ref, o_ref, xb_ref, *, scaled_tiles=None):
    @pl.when(pl.program_id(1) == 0)
    def _():
        xb_ref[...] = x_ref[...].astype(BF16)

    acc = _dot(xb_ref[...], w_ref[...].astype(BF16))
    scale = _tile_scale(scaled_tiles)
    o_ref[...] = (acc if scale is None else acc * scale).astype(o_ref.dtype)


def _mm_kernel_bf16(x_ref, w_ref, o_ref, *, scaled_tiles=None):
    acc = _dot(x_ref[...], w_ref[...].astype(BF16))
    scale = _tile_scale(scaled_tiles)
    o_ref[...] = (acc if scale is None else acc * scale).astype(o_ref.dtype)


def _matmul(x, w, layer, tm, tn, name, query_cols=()):
    m, k = x.shape
    n = w.shape[2]
    is_f32 = x.dtype == F32
    scaled_tiles = tuple(any(lo <= j * tn and (j + 1) * tn <= hi for lo, hi in query_cols)
                         for j in range(n // tn))
    assert sum(scaled_tiles) * tn == sum(hi - lo for lo, hi in query_cols)
    body = functools.partial(_mm_kernel_f32 if is_f32 else _mm_kernel_bf16, scaled_tiles=scaled_tiles)
    return pl.pallas_call(
        body,
        grid=(m // tm, n // tn),
        in_specs=[pl.BlockSpec((tm, k), lambda i, j: (i, 0)),
                  pl.BlockSpec((None, k, tn), lambda i, j: (layer, 0, j))],
        out_specs=pl.BlockSpec((tm, tn), lambda i, j: (i, j)),
        out_shape=jax.ShapeDtypeStruct((m, n), BF16),
        scratch_shapes=[pltpu.VMEM((tm, k), BF16)] if is_f32 else [],
        compiler_params=_params(("parallel", "arbitrary"), VMEM_LIMIT_BIG_TILES if is_f32 else VMEM_LIMIT),
        name=name,
    )(x, w)


def _matmul_all_layers(x, w, tn, name):
    m, k = x.shape
    nl, _, n = w.shape
    return pl.pallas_call(
        _mm_kernel_f32,
        grid=(nl, n // tn),
        in_specs=[pl.BlockSpec((m, k), lambda l, j: (0, 0)),
                  pl.BlockSpec((None, k, tn), lambda l, j: (l, 0, j))],
        out_specs=pl.BlockSpec((None, m, tn), lambda l, j: (l, 0, j)),
        out_shape=jax.ShapeDtypeStruct((nl, m, n), BF16),
        scratch_shapes=[pltpu.VMEM((m, k), BF16)],
        compiler_params=_params(("arbitrary", "arbitrary")),
        name=name,
    )(x, w)


def _kv_kernel(x_ref, w_ref, kv_ref, km_ref):
    acc = _dot(x_ref[...].astype(BF16), w_ref[...].astype(BF16))
    kv_ref[...] = acc.astype(kv_ref.dtype)
    nblk = acc.shape[0] // MOBA_BLOCK
    k = acc[:, :KV_W].reshape(nblk, MOBA_BLOCK, KV_W)
    km_ref[...] = jnp.mean(k, axis=1)[:, None, :]


def _shared_kv(x, w, tm):
    m, k = x.shape
    n = w.shape[1]
    nblk = tm // MOBA_BLOCK
    return pl.pallas_call(
        _kv_kernel,
        grid=(m // tm,),
        in_specs=[pl.BlockSpec((tm, k), lambda i: (i, 0)),
                  pl.BlockSpec((k, n), lambda i: (0, 0))],
        out_specs=[pl.BlockSpec((tm, n), lambda i: (i, 0)),
                   pl.BlockSpec((nblk, 1, KV_W), lambda i: (i, 0, 0))],
        out_shape=[jax.ShapeDtypeStruct((m, n), BF16),
                   jax.ShapeDtypeStruct((m // MOBA_BLOCK, 1, KV_W), F32)],
        compiler_params=_params(("parallel",)),
        name="shared_kv",
    )(x, w)


def _layer_norm(r, g, b):
    mu = jnp.mean(r, axis=-1, keepdims=True)
    c = r - mu
    var = jnp.mean(c * c, axis=-1, keepdims=True)
    return c * lax.rsqrt(var + LN_EPS) * g + b


def _memory_heads(qm_ref, mk_ref, mvt_ref, o_ref):
    tq = qm_ref.shape[1]
    step = min(tq, 2 * LANES)
    units = [(h, r0) for r0 in range(0, tq, step) for h in range(N_MEM_HEADS)]
    scores = [_dot_nt(mk_ref[0, :, _head(h)], qm_ref[0, r0:r0 + step, _head(h)])
              for h, r0 in units]
    probs, dens = [], []
    for s in scores:
        m = jnp.max(s, axis=0, keepdims=True)
        p = jnp.exp2(s - m)
        dens.append(jnp.sum(p, axis=0, keepdims=True))
        probs.append(p.astype(BF16))
    outs = [_dot(mvt_ref[0, _head(h), :], p) for (h, _), p in zip(units, probs)]
    for (h, r0), ot, l in zip(units, outs, dens):
        o_ref[0, r0:r0 + step, Q_W + h * HEAD_DIM:Q_W + (h + 1) * HEAD_DIM] = (ot / l).T.astype(o_ref.dtype)


def _swa_kernel(sink_ref, q_ref, kp_ref, kc_ref, vtp_ref, vtc_ref, qm_ref, mk_ref, mvt_ref, o_ref):
    tq = q_ref.shape[1]
    n = pl.program_id(1)
    sw = WINDOW
    c_io = lax.broadcasted_iota(jnp.int32, (2 * sw, sw), 0)
    r_io = lax.broadcasted_iota(jnp.int32, (2 * sw, sw), 1)
    dist_i = r_io + sw - c_io
    band = (dist_i >= 0) & (dist_i < sw)
    neg_dist = jnp.where(band, -dist_i.astype(F32), NEG_INF)
    first_row = jnp.where(n > 0, 0, sw)
    neg_dist_first = jnp.where(c_io >= first_row, neg_dist, NEG_INF)
    for sb in range(tq // sw):
        rows = slice(sb * sw, (sb + 1) * sw)
        nd = neg_dist_first if sb == 0 else neg_dist
        scores, vtws = [], []
        for kvh in range(N_KV_HEADS):
            cs = _head(kvh)
            if sb == 0:
                kw = jnp.concatenate([kp_ref[0, :, cs], kc_ref[0, :sw, cs]], axis=0)
                vtw = jnp.concatenate([vtp_ref[0, cs, :], vtc_ref[0, cs, :sw]], axis=1)
            else:
                kw = kc_ref[0, (sb - 1) * sw:(sb + 1) * sw, cs]
                vtw = vtc_ref[0, cs, (sb - 1) * sw:(sb + 1) * sw]
            q = jnp.concatenate([q_ref[0, rows, _head(kvh * GROUP + g)] for g in range(GROUP)], axis=0)
            scores.append(_dot_nt(kw, q))
            vtws.append(vtw)
        probs, denoms = [], []
        for h in range(N_Q_HEADS):
            kvh, g = divmod(h, GROUP)
            s = scores[kvh][:, g * sw:(g + 1) * sw] + SLOPES2[h] * nd
            sink = sink_ref[h] * LOG2E
            m = jnp.maximum(jnp.max(s, axis=0, keepdims=True), sink)
            p = jnp.exp2(s - m)
            denoms.append(jnp.sum(p, axis=0, keepdims=True) + jnp.exp2(sink - m))
            probs.append(p.astype(BF16))
        outs = [_dot(vtws[kvh], jnp.concatenate(probs[kvh * GROUP:(kvh + 1) * GROUP], axis=1))
                for kvh in range(N_KV_HEADS)]
        for h in range(N_Q_HEADS):
            kvh, g = divmod(h, GROUP)
            ot = outs[kvh][:, g * sw:(g + 1) * sw] / denoms[h]
            o_ref[0, rows, _head(h)] = ot.T.astype(o_ref.dtype)
    _memory_heads(qm_ref, mk_ref, mvt_ref, o_ref)


def _swa_layer(proj, vt, mkv, mvt, sinks, tq):
    b, t, _ = proj.shape
    mlen = mkv.shape[1]
    wpb = tq // WINDOW
    kcol = Q_W // KV_W
    prev = lambda n: jnp.maximum(n * wpb - 1, 0)
    return pl.pallas_call(
        _swa_kernel,
        grid=(b, t // tq),
        in_specs=[
            pl.BlockSpec(memory_space=pltpu.SMEM),
            pl.BlockSpec((1, tq, Q_W), lambda bi, n: (bi, n, 0)),
            pl.BlockSpec((1, WINDOW, KV_W), lambda bi, n: (bi, prev(n), kcol)),
            pl.BlockSpec((1, tq, KV_W), lambda bi, n: (bi, n, kcol)),
            pl.BlockSpec((1, KV_W, WINDOW), lambda bi, n: (bi, 0, prev(n))),
            pl.BlockSpec((1, KV_W, tq), lambda bi, n: (bi, 0, n)),
            pl.BlockSpec((1, tq, MQ_W), lambda bi, n: (bi, n, kcol + 2)),
            pl.BlockSpec((1, mlen, MQ_W), lambda bi, n: (bi, 0, 0)),
            pl.BlockSpec((1, MQ_W, mlen), lambda bi, n: (bi, 0, 0)),
        ],
        out_specs=pl.BlockSpec((1, tq, MIX_W), lambda bi, n: (bi, n, 0)),
        out_shape=jax.ShapeDtypeStruct((b, t, MIX_W), BF16),
        compiler_params=_params(("parallel", "arbitrary")),
        name="swa_mem_attn",
    )(sinks, proj, proj, proj, vt, vt, proj, mkv, mvt)


def _moba_kernel(q_ref, k_ref, vt_ref, km_ref, qm_ref, mk_ref, mvt_ref, o_ref,
                 al_ref, sel_ref, m_ref, l_ref, acc_ref):
    tq = MOBA_BLOCK
    nb = k_ref.shape[1]
    per_head = tq // LANES
    j = pl.program_id(1)
    key = lax.broadcasted_iota(jnp.int32, (tq, tq), 0)
    qry = lax.broadcasted_iota(jnp.int32, (tq, tq), 1)
    s_minus_t = (key - qry).astype(F32)
    causal_add = jnp.where(key <= qry, 0.0, NEG_INF)
    blk = lax.broadcasted_iota(jnp.int32, (nb, tq), 0)
    heads = range(N_Q_HEADS)

    @pl.when((pl.program_id(0) == 0) & (j == 0))
    def _():
        for h in heads:
            al_ref[h] = SLOPES2[h] * s_minus_t

    def head_lanes(h):
        return slice(h * tq, (h + 1) * tq)

    def scores_of(i):
        return [_dot_nt(k_ref[0, i, :, _head(h // GROUP)], q_ref[0, :, _head(h)]) for h in heads]

    def values_of(i, probs):
        return [_dot(vt_ref[0, i, _head(h // GROUP), :], probs[h]) for h in heads]

    for h in heads:
        kmb = km_ref[0, :, _head(h // GROUP)].astype(BF16)
        gate = jnp.where(blk < j, _dot_nt(kmb, q_ref[0, :, _head(h)]), NEG_INF)
        cnt = jnp.zeros((nb, tq), F32)
        for ip in range(nb):
            gi = gate[ip:ip + 1, :]
            beats = (gi > gate) | ((gi == gate) & (ip < blk))
            cnt = cnt + jnp.where(beats, 1.0, 0.0)
        sel = jnp.where((cnt < MOBA_TOPK) & (blk < j), 0.0, NEG_INF)
        for ip in range(nb):
            sel_ref[ip, :, head_lanes(h)] = sel[ip:ip + 1, :]
    slope_row = jnp.concatenate([jnp.full((1, tq), SLOPES2[h], F32) for h in heads], axis=1)

    raw = scores_of(j)
    probs = []
    for h in heads:
        ps = []
        for half in range(per_head):
            lanes, hl = _lane_chunk(h * per_head + half), _lane_chunk(half)
            s = raw[h][:, hl] + (al_ref[h, :, hl] + causal_add[:, hl])
            m0 = jnp.max(s, axis=0, keepdims=True)
            p = jnp.exp2(s - m0)
            m_ref[:, lanes] = m0
            l_ref[:, lanes] = jnp.sum(p, axis=0, keepdims=True)
            ps.append(p.astype(BF16))
        probs.append(jnp.concatenate(ps, axis=1))
    for h, pv in zip(heads, values_of(j, probs)):
        acc_ref[:, head_lanes(h)] = pv

    def past_block(i, carry):
        gap = ((j - i) * tq).astype(F32)
        rowc = sel_ref[i] - slope_row * gap
        raw = scores_of(i)
        probs, scales = [], []
        for h in heads:
            ps, sc = [], []
            for half in range(per_head):
                lanes, hl = _lane_chunk(h * per_head + half), _lane_chunk(half)
                s = raw[h][:, hl] + al_ref[h, :, hl]
                m_old = m_ref[:, lanes]
                m_new = jnp.maximum(m_old, jnp.max(s, axis=0, keepdims=True) + rowc[:, lanes])
                a = jnp.exp2(m_old - m_new)
                p = jnp.exp2(s - (m_new - rowc[:, lanes]))
                l_ref[:, lanes] = a * l_ref[:, lanes] + jnp.sum(p, axis=0, keepdims=True)
                m_ref[:, lanes] = m_new
                ps.append(p.astype(BF16))
                sc.append(a)
            probs.append(jnp.concatenate(ps, axis=1))
            scales.append(jnp.concatenate(sc, axis=1))
        for h, pv in zip(heads, values_of(i, probs)):
            acc_ref[:, head_lanes(h)] = scales[h] * acc_ref[:, head_lanes(h)] + pv
        return carry

    lax.fori_loop(0, j, past_block, 0)
    for h in heads:
        ot = acc_ref[:, head_lanes(h)] / l_ref[:, head_lanes(h)]
        o_ref[0, :, _head(h)] = ot.T.astype(o_ref.dtype)
    _memory_heads(qm_ref, mk_ref, mvt_ref, o_ref)


def _moba_layer(proj, k4, vt4, kmeans, mkv, mvt):
    b, t, _ = proj.shape
    mlen = mkv.shape[1]
    tq = MOBA_BLOCK
    nb = t // tq
    hq = N_Q_HEADS * tq
    return pl.pallas_call(
        _moba_kernel,
        grid=(b, nb),
        in_specs=[
            pl.BlockSpec((1, tq, Q_W), lambda bi, n: (bi, n, 0)),
            pl.BlockSpec((1, nb, tq, KV_W), lambda bi, n: (bi, 0, 0, 0)),
            pl.BlockSpec((1, nb, KV_W, tq), lambda bi, n: (bi, 0, 0, 0)),
            pl.BlockSpec((1, nb, KV_W), lambda bi, n: (bi, 0, 0)),
            pl.BlockSpec((1, tq, MQ_W), lambda bi, n: (bi, n, Q_W // MQ_W)),
            pl.BlockSpec((1, mlen, MQ_W), lambda bi, n: (bi, 0, 0)),
            pl.BlockSpec((1, MQ_W, mlen), lambda bi, n: (bi, 0, 0)),
        ],
        out_specs=pl.BlockSpec((1, tq, MIX_W), lambda bi, n: (bi, n, 0)),
        out_shape=jax.ShapeDtypeStruct((b, t, MIX_W), BF16),
        scratch_shapes=[pltpu.VMEM((N_Q_HEADS, tq, tq), F32),
                        pltpu.VMEM((nb, 1, hq), F32),
                        pltpu.VMEM((1, hq), F32),
                        pltpu.VMEM((1, hq), F32),
                        pltpu.VMEM((HEAD_DIM, hq), F32)],
        compiler_params=_params(("arbitrary", "arbitrary")),
        name="moba_mem_attn",
    )(proj, k4, vt4, kmeans, proj, mkv, mvt)


OUT_ROW_CHUNK = 256


def _out_proj_kernel(mix_ref, w_ref, h_ref, g_ref, b_ref, o_ref):
    for r0 in range(0, o_ref.shape[0], OUT_ROW_CHUNK):
        rows = slice(r0, r0 + OUT_ROW_CHUNK)
        r = ALPHA * h_ref[rows, :] + _dot(mix_ref[rows, :], w_ref[...])
        o_ref[rows, :] = _layer_norm(r, g_ref[...], b_ref[...])


def _ln_specs(ln_index):
    spec = pl.BlockSpec((None, 1, D_MODEL), lambda *_: (ln_index, 0, 0))
    return [spec, spec]


def _out_proj_ln(mix, w, layer, h, ln_g, ln_b, tm):
    m, k = mix.shape
    n = w.shape[2]
    return pl.pallas_call(
        _out_proj_kernel,
        grid=(m // tm,),
        in_specs=[pl.BlockSpec((tm, k), lambda i: (i, 0)),
                  pl.BlockSpec((None, k, n), lambda i: (layer, 0, 0), pipeline_mode=pl.Buffered(1)),
                  pl.BlockSpec((tm, n), lambda i: (i, 0))] + _ln_specs(2 * layer),
        out_specs=pl.BlockSpec((tm, n), lambda i: (i, 0)),
        out_shape=jax.ShapeDtypeStruct((m, n), F32),
        compiler_params=_params(("parallel",)),
        name="out_proj_ln",
    )(mix, w, h, ln_g, ln_b)


def _ffn_kernel(h_ref, wu_ref, wd_ref, g_ref, b_ref, o_ref, ob_ref):
    f = pl.program_id(1)

    @pl.when(f == 0)
    def _():
        ob_ref[...] = h_ref[...].astype(BF16)
        o_ref[...] = ALPHA * h_ref[...]

    u = jnp.maximum(_dot(ob_ref[...], wu_ref[...].astype(BF16)), 0.0)
    o_ref[...] += _dot((u * u).astype(BF16), wd_ref[...].astype(BF16))

    @pl.when(f == pl.num_programs(1) - 1)
    def _():
        y = _layer_norm(o_ref[...], g_ref[...], b_ref[...])
        o_ref[...] = y
        ob_ref[...] = y.astype(BF16)


def _ffn_ln(h, w_up, w_down, layer, ln_g, ln_b, tm, tf):
    m, d = h.shape
    dff = w_up.shape[2]
    return pl.pallas_call(
        _ffn_kernel,
        grid=(m // tm, dff // tf),
        in_specs=[pl.BlockSpec((tm, d), lambda i, f: (i, 0)),
                  pl.BlockSpec((None, d, tf), lambda i, f: (layer, 0, f)),
                  pl.BlockSpec((None, tf, d), lambda i, f: (layer, f, 0))] + _ln_specs(2 * layer + 1),
        out_specs=[pl.BlockSpec((tm, d), lambda i, f: (i, 0)),
                   pl.BlockSpec((tm, d), lambda i, f: (i, 0))],
        out_shape=[jax.ShapeDtypeStruct((m, d), F32), jax.ShapeDtypeStruct((m, d), BF16)],
        compiler_params=_params(("parallel", "arbitrary"), VMEM_LIMIT_BIG_TILES),
        name="ffn_ln",
    )(h, w_up, w_down, ln_g, ln_b)


PROJ_TM_F32, PROJ_TM_BF16, PROJ_TN = 2048, 2048, 512
KV_TM = 1024
OUT_TM = 1024
FFN_TM, FFN_TF = 1024, 512
SWA_TQ = 1024


def kernel(x, mem, w_in_a, sinks_a, w_q_b, w_kv_shared, w_mem_kv, w_o, w_up, w_down, ln_g, ln_b):
    b, t, d = x.shape
    mlen = mem.shape[1]
    m = b * t
    nb = t // MOBA_BLOCK
    h = x.reshape(m, d)
    mem2 = mem.reshape(b * mlen, d)
    w_o_bf = w_o.astype(BF16)
    ln_g2 = ln_g.reshape(2 * DEPTH, 1, d)
    ln_b2 = ln_b.reshape(2 * DEPTH, 1, d)
    mkv_all = _matmul_all_layers(mem2, w_mem_kv, 512, "mem_kv").reshape(DEPTH, b, mlen, 2 * MQ_W)
    mvt_all = jnp.swapaxes(mkv_all[..., MQ_W:], 2, 3)
    k4 = vt4 = kmeans = None
    hx = h
    for layer in range(DEPTH):
        proj_tm = PROJ_TM_F32 if hx.dtype == F32 else PROJ_TM_BF16
        mkv, mvt = mkv_all[layer], mvt_all[layer]
        if layer < N_A_LAYERS:
            proj = _matmul(hx, w_in_a, layer, proj_tm, PROJ_TN, "proj_a",
                           query_cols=((0, Q_W), (Q_W + 2 * KV_W, Q_W + 2 * KV_W + MQ_W))).reshape(b, t, -1)
            vt = jnp.swapaxes(proj[:, :, Q_W + KV_W:Q_W + 2 * KV_W], 1, 2)
            mix = _swa_layer(proj, vt, mkv, mvt, sinks_a[layer], SWA_TQ)
        else:
            if k4 is None:
                kv, km = _shared_kv(hx, w_kv_shared, KV_TM)
                k4 = kv.reshape(b, nb, MOBA_BLOCK, 2 * KV_W)
                vt4 = jnp.swapaxes(k4[..., KV_W:], 2, 3)
                kmeans = km.reshape(b, nb, KV_W)
            proj = _matmul(hx, w_q_b, layer - N_A_LAYERS, proj_tm, PROJ_TN, "proj_b", query_cols=((0, MIX_W),))
            mix = _moba_layer(proj.reshape(b, t, -1), k4, vt4, kmeans, mkv, mvt)
        h = _out_proj_ln(mix.reshape(m, MIX_W), w_o_bf, layer, h, ln_g2, ln_b2, OUT_TM)
        h, hx = _ffn_ln(h, w_up, w_down, layer, ln_g2, ln_b2, FFN_TM, FFN_TF)
    return h.reshape(b, t, d)
```

```python
import functools
import math

import jax
import jax.numpy as jnp
from jax import lax
from jax.experimental import pallas as pl
from jax.experimental.pallas import tpu as pltpu

D_MODEL = 2048
DEPTH = 4
HEAD_DIM = 128
N_Q_HEADS = 12
N_KV_HEADS = 4
GROUP = N_Q_HEADS // N_KV_HEADS
N_MEM_HEADS = 4
WINDOW = 128
MOBA_BLOCK = 256
MOBA_TOPK = 3
D_FF = 4 * D_MODEL
N_A_LAYERS = DEPTH // 2
ALPHA = (2 * DEPTH) ** 0.25
LN_EPS = 1e-5
NEG_INF = -1e30
Q_W = N_Q_HEADS * HEAD_DIM
KV_W = N_KV_HEADS * HEAD_DIM
MQ_W = N_MEM_HEADS * HEAD_DIM
MIX_W = Q_W + MQ_W
SCALE = HEAD_DIM ** -0.5
SLOPES = tuple(2.0 ** (-8.0 * h / N_Q_HEADS) for h in range(1, N_Q_HEADS + 1))
LOG2E = math.log2(math.e)
SCALE2 = SCALE * LOG2E
SLOPES2 = tuple(s * LOG2E for s in SLOPES)

V7X_VMEM_BYTES = 64 * 1024 * 1024
VMEM_LIMIT = V7X_VMEM_BYTES - 8 * 1024 * 1024
LANES = 128

BF16 = jnp.bfloat16
F32 = jnp.float32


VMEM_LIMIT_BIG_TILES = V7X_VMEM_BYTES - 1024 * 1024


def _params(sem, vmem_limit=VMEM_LIMIT):
    return pltpu.CompilerParams(dimension_semantics=sem, vmem_limit_bytes=vmem_limit)


def _dot(a, b):
    return jnp.dot(a, b, preferred_element_type=F32)


def _dot_nt(a, b):
    return lax.dot_general(a, b, (((1,), (1,)), ((), ())), preferred_element_type=F32)


def _head(h):
    return slice(h * HEAD_DIM, (h + 1) * HEAD_DIM)


def _lane_chunk(c):
    return slice(c * LANES, (c + 1) * LANES)


def _tile_scale(scaled_tiles):
    if scaled_tiles is None or not any(scaled_tiles):
        return None
    if all(scaled_tiles):
        return SCALE2
    j = pl.program_id(1)
    is_scaled = functools.reduce(jnp.logical_or, [j == t for t, on in enumerate(scaled_tiles) if on])
    return jnp.where(is_scaled, SCALE2, 1.0)


def _mm_kernel_f32(x_ref, w_ref, o_ref, xb_ref, *, scaled_tiles=None):
    @pl.when(pl.program_id(1) == 0)
    def _():
        xb_ref[...] = x_ref[...].astype(BF16)

    acc = _dot(xb_ref[...], w_ref[...].astype(BF16))
    scale = _tile_scale(scaled_tiles)
    o_ref[...] = (acc if scale is None else acc * scale).astype(o_ref.dtype)


def _mm_kernel_bf16(x_ref, w_ref, o_ref, *, scaled_tiles=None):
    acc = _dot(x_ref[...], w_ref[...].astype(BF16))
    scale = _tile_scale(scaled_tiles)
    o_ref[...] = (acc if scale is None else acc * scale).astype(o_ref.dtype)


def _matmul(x, w, layer, tm, tn, name, query_cols=()):
    m, k = x.shape
    n = w.shape[2]
    is_f32 = x.dtype == F32
    scaled_tiles = tuple(any(lo <= j * tn and (j + 1) * tn <= hi for lo, hi in query_cols)
                         for j in range(n // tn))
    assert sum(scaled_tiles) * tn == sum(hi - lo for lo, hi in query_cols)
    body = functools.partial(_mm_kernel_f32 if is_f32 else _mm_kernel_bf16, scaled_tiles=scaled_tiles)
    return pl.pallas_call(
        body,
        grid=(m // tm, n // tn),
        in_specs=[pl.BlockSpec((tm, k), lambda i, j: (i, 0)),
                  pl.BlockSpec((None, k, tn), lambda i, j: (layer, 0, j))],
        out_specs=pl.BlockSpec((tm, tn), lambda i, j: (i, j)),
        out_shape=jax.ShapeDtypeStruct((m, n), BF16),
        scratch_shapes=[pltpu.VMEM((tm, k), BF16)] if is_f32 else [],
        compiler_params=_params(("parallel", "arbitrary"), VMEM_LIMIT_BIG_TILES),
        name=name,
    )(x, w)


def _matmul_all_layers(x, w, tn, name):
    m, k = x.shape
    nl, _, n = w.shape
    return pl.pallas_call(
        _mm_kernel_f32,
        grid=(nl, n // tn),
        in_specs=[pl.BlockSpec((m, k), lambda l, j: (0, 0)),
                  pl.BlockSpec((None, k, tn), lambda l, j: (l, 0, j))],
        out_specs=pl.BlockSpec((None, m, tn), lambda l, j: (l, 0, j)),
        out_shape=jax.ShapeDtypeStruct((nl, m, n), BF16),
        scratch_shapes=[pltpu.VMEM((m, k), BF16)],
        compiler_params=_params(("arbitrary", "arbitrary")),
        name=name,
    )(x, w)


def _kv_kernel(x_ref, w_ref, kv_ref, km_ref):
    acc = _dot(x_ref[...].astype(BF16), w_ref[...].astype(BF16))
    kv_ref[...] = acc.astype(kv_ref.dtype)
    nblk = acc.shape[0] // MOBA_BLOCK
    k = acc[:, :KV_W].reshape(nblk, MOBA_BLOCK, KV_W)
    km_ref[...] = jnp.mean(k, axis=1)[:, None, :]


def _shared_kv(x, w, tm):
    m, k = x.shape
    n = w.shape[1]
    nblk = tm // MOBA_BLOCK
    return pl.pallas_call(
        _kv_kernel,
        grid=(m // tm,),
        in_specs=[pl.BlockSpec((tm, k), lambda i: (i, 0)),
                  pl.BlockSpec((k, n), lambda i: (0, 0))],
        out_specs=[pl.BlockSpec((tm, n), lambda i: (i, 0)),
                   pl.BlockSpec((nblk, 1, KV_W), lambda i: (i, 0, 0))],
        out_shape=[jax.ShapeDtypeStruct((m, n), BF16),
                   jax.ShapeDtypeStruct((m // MOBA_BLOCK, 1, KV_W), F32)],
        compiler_params=_params(("parallel",)),
        name="shared_kv",
    )(x, w)


def _layer_norm(r, g, b):
    mu = jnp.mean(r, axis=-1, keepdims=True)
    c = r - mu
    var = jnp.mean(c * c, axis=-1, keepdims=True)
    return c * lax.rsqrt(var + LN_EPS) * g + b


def _memory_heads(qm_ref, mk_ref, mvt_ref, o_ref):
    tq = qm_ref.shape[1]
    step = min(tq, 2 * LANES)
    units = [(h, r0) for r0 in range(0, tq, step) for h in range(N_MEM_HEADS)]
    scores = [_dot_nt(mk_ref[0, :, _head(h)], qm_ref[0, r0:r0 + step, _head(h)])
              for h, r0 in units]
    probs, dens = [], []
    for s in scores:
        m = jnp.max(s, axis=0, keepdims=True)
        p = jnp.exp2(s - m)
        dens.append(jnp.sum(p, axis=0, keepdims=True))
        probs.append(p.astype(BF16))
    outs = [_dot(mvt_ref[0, _head(h), :], p) for (h, _), p in zip(units, probs)]
    for (h, r0), ot, l in zip(units, outs, dens):
        o_ref[0, r0:r0 + step, Q_W + h * HEAD_DIM:Q_W + (h + 1) * HEAD_DIM] = (ot / l).T.astype(o_ref.dtype)


def _swa_kernel(sink_ref, q_ref, kp_ref, kc_ref, vtp_ref, vtc_ref, qm_ref, mk_ref, mvt_ref, o_ref):
    tq = q_ref.shape[1]
    n = pl.program_id(1)
    sw = WINDOW
    c_io = lax.broadcasted_iota(jnp.int32, (2 * sw, sw), 0)
    r_io = lax.broadcasted_iota(jnp.int32, (2 * sw, sw), 1)
    dist_i = r_io + sw - c_io
    band = (dist_i >= 0) & (dist_i < sw)
    neg_dist = jnp.where(band, -dist_i.astype(F32), NEG_INF)
    first_row = jnp.where(n > 0, 0, sw)
    neg_dist_first = jnp.where(c_io >= first_row, neg_dist, NEG_INF)
    for sb in range(tq // sw):
        rows = slice(sb * sw, (sb + 1) * sw)
        nd = neg_dist_first if sb == 0 else neg_dist
        scores, vtws = [], []
        for kvh in range(N_KV_HEADS):
            cs = _head(kvh)
            if sb == 0:
                kw = jnp.concatenate([kp_ref[0, :, cs], kc_ref[0, :sw, cs]], axis=0)
                vtw = jnp.concatenate([vtp_ref[0, cs, :], vtc_ref[0, cs, :sw]], axis=1)
            else:
                kw = kc_ref[0, (sb - 1) * sw:(sb + 1) * sw, cs]
                vtw = vtc_ref[0, cs, (sb - 1) * sw:(sb + 1) * sw]
            q = jnp.concatenate([q_ref[0, rows, _head(kvh * GROUP + g)] for g in range(GROUP)], axis=0)
            scores.append(_dot_nt(kw, q))
            vtws.append(vtw)
        probs, denoms = [], []
        for h in range(N_Q_HEADS):
            kvh, g = divmod(h, GROUP)
            s = scores[kvh][:, g * sw:(g + 1) * sw] + SLOPES2[h] * nd
            sink = sink_ref[h] * LOG2E
            m = jnp.maximum(jnp.max(s, axis=0, keepdims=True), sink)
            p = jnp.exp2(s - m)
            denoms.append(jnp.sum(p, axis=0, keepdims=True) + jnp.exp2(sink - m))
            probs.append(p.astype(BF16))
        outs = [_dot(vtws[kvh], jnp.concatenate(probs[kvh * GROUP:(kvh + 1) * GROUP], axis=1))
                for kvh in range(N_KV_HEADS)]
        for h in range(N_Q_HEADS):
            kvh, g = divmod(h, GROUP)
            ot = outs[kvh][:, g * sw:(g + 1) * sw] / denoms[h]
            o_ref[0, rows, _head(h)] = ot.T.astype(o_ref.dtype)
    _memory_heads(qm_ref, mk_ref, mvt_ref, o_ref)


def _swa_layer(proj, vt, mkv, mvt, sinks, tq):
    b, t, _ = proj.shape
    mlen = mkv.shape[1]
    wpb = tq // WINDOW
    kcol = Q_W // KV_W
    prev = lambda n: jnp.maximum(n * wpb - 1, 0)
    return pl.pallas_call(
        _swa_kernel,
        grid=(b, t // tq),
        in_specs=[
            pl.BlockSpec(memory_space=pltpu.SMEM),
            pl.BlockSpec((1, tq, Q_W), lambda bi, n: (bi, n, 0)),
            pl.BlockSpec((1, WINDOW, KV_W), lambda bi, n: (bi, prev(n), kcol)),
            pl.BlockSpec((1, tq, KV_W), lambda bi, n: (bi, n, kcol)),
            pl.BlockSpec((1, KV_W, WINDOW), lambda bi, n: (bi, 0, prev(n))),
            pl.BlockSpec((1, KV_W, tq), lambda bi, n: (bi, 0, n)),
            pl.BlockSpec((1, tq, MQ_W), lambda bi, n: (bi, n, kcol + 2)),
            pl.BlockSpec((1, mlen, MQ_W), lambda bi, n: (bi, 0, 0)),
            pl.BlockSpec((1, MQ_W, mlen), lambda bi, n: (bi, 0, 0)),
        ],
        out_specs=pl.BlockSpec((1, tq, MIX_W), lambda bi, n: (bi, n, 0)),
        out_shape=jax.ShapeDtypeStruct((b, t, MIX_W), BF16),
        compiler_params=_params(("parallel", "arbitrary")),
        name="swa_mem_attn",
    )(sinks, proj, proj, proj, vt, vt, proj, mkv, mvt)


def _moba_kernel(q_ref, k_ref, vt_ref, km_ref, qm_ref, mk_ref, mvt_ref, o_ref,
                 al_ref, sel_ref, m_ref, l_ref, acc_ref):
    tq = MOBA_BLOCK
    nb = k_ref.shape[1]
    per_head = tq // LANES
    j = pl.program_id(1)
    key = lax.broadcasted_iota(jnp.int32, (tq, tq), 0)
    qry = lax.broadcasted_iota(jnp.int32, (tq, tq), 1)
    s_minus_t = (key - qry).astype(F32)
    causal_add = jnp.where(key <= qry, 0.0, NEG_INF)
    blk = lax.broadcasted_iota(jnp.int32, (nb, tq), 0)
    heads = range(N_Q_HEADS)

    @pl.when((pl.program_id(0) == 0) & (j == 0))
    def _():
        for h in heads:
            al_ref[h] = SLOPES2[h] * s_minus_t

    def head_lanes(h):
        return slice(h * tq, (h + 1) * tq)

    def scores_of(i):
        return [_dot_nt(k_ref[0, i, :, _head(h // GROUP)], q_ref[0, :, _head(h)]) for h in heads]

    def values_of(i, probs):
        return [_dot(vt_ref[0, i, _head(h // GROUP), :], probs[h]) for h in heads]

    for h in heads:
        kmb = km_ref[0, :, _head(h // GROUP)].astype(BF16)
        gate = jnp.where(blk < j, _dot_nt(kmb, q_ref[0, :, _head(h)]), NEG_INF)
        cnt = jnp.zeros((nb, tq), F32)
        for ip in range(nb):
            gi = gate[ip:ip + 1, :]
            beats = (gi > gate) | ((gi == gate) & (ip < blk))
            cnt = cnt + jnp.where(beats, 1.0, 0.0)
        sel = jnp.where((cnt < MOBA_TOPK) & (blk < j), 0.0, NEG_INF)
        for ip in range(nb):
            sel_ref[ip, :, head_lanes(h)] = sel[ip:ip + 1, :]
    slope_row = jnp.concatenate([jnp.full((1, tq), SLOPES2[h], F32) for h in heads], axis=1)

    raw = scores_of(j)
    probs = []
    for h in heads:
        ps = []
        for half in range(per_head):
            lanes, hl = _lane_chunk(h * per_head + half), _lane_chunk(half)
            s = raw[h][:, hl] + (al_ref[h, :, hl] + causal_add[:, hl])
            m0 = jnp.max(s, axis=0, keepdims=True)
            p = jnp.exp2(s - m0)
            m_ref[:, lanes] = m0
            l_ref[:, lanes] = jnp.sum(p, axis=0, keepdims=True)
            ps.append(p.astype(BF16))
        probs.append(jnp.concatenate(ps, axis=1))
    for h, pv in zip(heads, values_of(j, probs)):
        acc_ref[:, head_lanes(h)] = pv

    def past_block(i, carry):
        gap = ((j - i) * tq).astype(F32)
        rowc = sel_ref[i] - slope_row * gap
        raw = scores_of(i)
        probs, scales = [], []
        for h in heads:
            ps, sc = [], []
            for half in range(per_head):
                lanes, hl = _lane_chunk(h * per_head + half), _lane_chunk(half)
                s = raw[h][:, hl] + al_ref[h, :, hl]
                m_old = m_ref[:, lanes]
                m_new = jnp.maximum(m_old, jnp.max(s, axis=0, keepdims=True) + rowc[:, lanes])
                a = jnp.exp2(m_old - m_new)
                p = jnp.exp2(s - (m_new - rowc[:, lanes]))
                l_ref[:, lanes] = a * l_ref[:, lanes] + jnp.sum(p, axis=0, keepdims=True)
                m_ref[:, lanes] = m_new
                ps.append(p.astype(BF16))
                sc.append(a)
            probs.append(jnp.concatenate(ps, axis=1))
            scales.append(jnp.concatenate(sc, axis=1))
        for h, pv in zip(heads, values_of(i, probs)):
            acc_ref[:, head_lanes(h)] = scales[h] * acc_ref[:, head_lanes(h)] + pv
        return carry

    lax.fori_loop(0, j, past_block, 0)
    for h in heads:
        ot = acc_ref[:, head_lanes(h)] / l_ref[:, head_lanes(h)]
        o_ref[0, :, _head(h)] = ot.T.astype(o_ref.dtype)
    _memory_heads(qm_ref, mk_ref, mvt_ref, o_ref)


def _moba_layer(proj, k4, vt4, kmeans, mkv, mvt):
    b, t, _ = proj.shape
    mlen = mkv.shape[1]
    tq = MOBA_BLOCK
    nb = t // tq
    hq = N_Q_HEADS * tq
    return pl.pallas_call(
        _moba_kernel,
        grid=(b, nb),
        in_specs=[
            pl.BlockSpec((1, tq, Q_W), lambda bi, n: (bi, n, 0)),
            pl.BlockSpec((1, nb, tq, KV_W), lambda bi, n: (bi, 0, 0, 0)),
            pl.BlockSpec((1, nb, KV_W, tq), lambda bi, n: (bi, 0, 0, 0)),
            pl.BlockSpec((1, nb, KV_W), lambda bi, n: (bi, 0, 0)),
            pl.BlockSpec((1, tq, MQ_W), lambda bi, n: (bi, n, Q_W // MQ_W)),
            pl.BlockSpec((1, mlen, MQ_W), lambda bi, n: (bi, 0, 0)),
            pl.BlockSpec((1, MQ_W, mlen), lambda bi, n: (bi, 0, 0)),
        ],
        out_specs=pl.BlockSpec((1, tq, MIX_W), lambda bi, n: (bi, n, 0)),
        out_shape=jax.ShapeDtypeStruct((b, t, MIX_W), BF16),
        scratch_shapes=[pltpu.VMEM((N_Q_HEADS, tq, tq), F32),
                        pltpu.VMEM((nb, 1, hq), F32),
                        pltpu.VMEM((1, hq), F32),
                        pltpu.VMEM((1, hq), F32),
                        pltpu.VMEM((HEAD_DIM, hq), F32)],
        compiler_params=_params(("arbitrary", "arbitrary")),
        name="moba_mem_attn",
    )(proj, k4, vt4, kmeans, proj, mkv, mvt)


OUT_ROW_CHUNK = 256


def _out_proj_kernel(mix_ref, w_ref, h_ref, g_ref, b_ref, o_ref):
    for r0 in range(0, o_ref.shape[0], OUT_ROW_CHUNK):
        rows = slice(r0, r0 + OUT_ROW_CHUNK)
        r = ALPHA * h_ref[rows, :] + _dot(mix_ref[rows, :], w_ref[...])
        o_ref[rows, :] = _layer_norm(r, g_ref[...], b_ref[...])


def _ln_specs(ln_index):
    spec = pl.BlockSpec((None, 1, D_MODEL), lambda *_: (ln_index, 0, 0))
    return [spec, spec]


def _out_proj_ln(mix, w, layer, h, ln_g, ln_b, tm):
    m, k = mix.shape
    n = w.shape[2]
    return pl.pallas_call(
        _out_proj_kernel,
        grid=(m // tm,),
        in_specs=[pl.BlockSpec((tm, k), lambda i: (i, 0)),
                  pl.BlockSpec((None, k, n), lambda i: (layer, 0, 0), pipeline_mode=pl.Buffered(1)),
                  pl.BlockSpec((tm, n), lambda i: (i, 0))] + _ln_specs(2 * layer),
        out_specs=pl.BlockSpec((tm, n), lambda i: (i, 0)),
        out_shape=jax.ShapeDtypeStruct((m, n), F32),
        compiler_params=_params(("parallel",)),
        name="out_proj_ln",
    )(mix, w, h, ln_g, ln_b)


def _ffn_kernel(h_ref, wu_ref, wd_ref, g_ref, b_ref, o_ref, ob_ref):
    f = pl.program_id(1)

    @pl.when(f == 0)
    def _():
        ob_ref[...] = h_ref[...].astype(BF16)
        o_ref[...] = ALPHA * h_ref[...]

    u = jnp.maximum(_dot(ob_ref[...], wu_ref[...].astype(BF16)), 0.0)
    o_ref[...] += _dot((u * u).astype(BF16), wd_ref[...].astype(BF16))

    @pl.when(f == pl.num_programs(1) - 1)
    def _():
        y = _layer_norm(o_ref[...], g_ref[...], b_ref[...])
        o_ref[...] = y
        ob_ref[...] = y.astype(BF16)


def _ffn_ln(h, w_up, w_down, layer, ln_g, ln_b, tm, tf, emit_bf16):
    m, d = h.shape
    dff = w_up.shape[2]
    row_spec = pl.BlockSpec((tm, d), lambda i, f: (i, 0))
    outs = pl.pallas_call(
        _ffn_kernel,
        grid=(m // tm, dff // tf),
        in_specs=[row_spec,
                  pl.BlockSpec((None, d, tf), lambda i, f: (layer, 0, f)),
                  pl.BlockSpec((None, tf, d), lambda i, f: (layer, f, 0))] + _ln_specs(2 * layer + 1),
        out_specs=[row_spec, row_spec] if emit_bf16 else [row_spec],
        out_shape=[jax.ShapeDtypeStruct((m, d), F32)] + ([jax.ShapeDtypeStruct((m, d), BF16)] if emit_bf16 else []),
        scratch_shapes=[] if emit_bf16 else [pltpu.VMEM((tm, d), BF16)],
        compiler_params=_params(("parallel", "arbitrary"), VMEM_LIMIT_BIG_TILES),
        name="ffn_ln",
    )(h, w_up, w_down, ln_g, ln_b)
    return (outs[0], outs[1]) if emit_bf16 else (outs[0], None)


PROJ_TM_F32, PROJ_TM_BF16 = 2048, 2048
PROJ_A_TN, PROJ_B_TN = 512, 1024
KV_TM = 1024
OUT_TM = 512
FFN_TM, FFN_TF = 1024, 512
SWA_TQ = 1024


def kernel(x, mem, w_in_a, sinks_a, w_q_b, w_kv_shared, w_mem_kv, w_o, w_up, w_down, ln_g, ln_b):
    b, t, d = x.shape
    mlen = mem.shape[1]
    m = b * t
    nb = t // MOBA_BLOCK
    h = x.reshape(m, d)
    mem2 = mem.reshape(b * mlen, d)
    w_o_bf = w_o.astype(BF16)
    ln_g2 = ln_g.reshape(2 * DEPTH, 1, d)
    ln_b2 = ln_b.reshape(2 * DEPTH, 1, d)
    mkv_all = _matmul_all_layers(mem2, w_mem_kv, 512, "mem_kv").reshape(DEPTH, b, mlen, 2 * MQ_W)
    mvt_all = jnp.swapaxes(mkv_all[..., MQ_W:], 2, 3)
    k4 = vt4 = kmeans = None
    hx = h
    for layer in range(DEPTH):
        proj_tm = PROJ_TM_F32 if hx.dtype == F32 else PROJ_TM_BF16
        mkv, mvt = mkv_all[layer], mvt_all[layer]
        if layer < N_A_LAYERS:
            proj = _matmul(hx, w_in_a, layer, proj_tm, PROJ_A_TN, "proj_a",
                           query_cols=((0, Q_W), (Q_W + 2 * KV_W, Q_W + 2 * KV_W + MQ_W))).reshape(b, t, -1)
            vt = jnp.swapaxes(proj[:, :, Q_W + KV_W:Q_W + 2 * KV_W], 1, 2)
            mix = _swa_layer(proj, vt, mkv, mvt, sinks_a[layer], SWA_TQ)
        else:
            if k4 is None:
                kv, km = _shared_kv(hx, w_kv_shared, KV_TM)
                k4 = kv.reshape(b, nb, MOBA_BLOCK, 2 * KV_W)
                vt4 = jnp.swapaxes(k4[..., KV_W:], 2, 3)
                kmeans = km.reshape(b, nb, KV_W)
            proj = _matmul(hx, w_q_b, layer - N_A_LAYERS, proj_tm, PROJ_B_TN, "proj_b", query_cols=((0, MIX_W),))
            mix = _moba_layer(proj.reshape(b, t, -1), k4, vt4, kmeans, mkv, mvt)
        h = _out_proj_ln(mix.reshape(m, MIX_W), w_o_bf, layer, h, ln_g2, ln_b2, OUT_TM)
        h, hx = _ffn_ln(h, w_up, w_down, layer, ln_g2, ln_b2, FFN_TM, FFN_TF, emit_bf16=layer + 1 < DEPTH)
    return h.reshape(b, t, d)
```

```python
import functools
import math

import jax
import jax.numpy as jnp
from jax import lax
from jax.experimental import pallas as pl
from jax.experimental.pallas import tpu as pltpu

D_MODEL = 2048
DEPTH = 4
HEAD_DIM = 128
N_Q_HEADS = 12
N_KV_HEADS = 4
GROUP = N_Q_HEADS // N_KV_HEADS
N_MEM_HEADS = 4
WINDOW = 128
MOBA_BLOCK = 256
MOBA_TOPK = 3
D_FF = 4 * D_MODEL
N_A_LAYERS = DEPTH // 2
ALPHA = (2 * DEPTH) ** 0.25
LN_EPS = 1e-5
NEG_INF = -1e30
Q_W = N_Q_HEADS * HEAD_DIM
KV_W = N_KV_HEADS * HEAD_DIM
MQ_W = N_MEM_HEADS * HEAD_DIM
MIX_W = Q_W + MQ_W
SCALE = HEAD_DIM ** -0.5
SLOPES = tuple(2.0 ** (-8.0 * h / N_Q_HEADS) for h in range(1, N_Q_HEADS + 1))
LOG2E = math.log2(math.e)
SCALE2 = SCALE * LOG2E
SLOPES2 = tuple(s * LOG2E for s in SLOPES)

V7X_VMEM_BYTES = 64 * 1024 * 1024
VMEM_LIMIT = V7X_VMEM_BYTES - 8 * 1024 * 1024
LANES = 128

BF16 = jnp.bfloat16
F32 = jnp.float32


VMEM_LIMIT_BIG_TILES = V7X_VMEM_BYTES - 1024 * 1024


def _params(sem, vmem_limit=VMEM_LIMIT):
    return pltpu.CompilerParams(dimension_semantics=sem, vmem_limit_bytes=vmem_limit)


def _dot(a, b):
    return jnp.dot(a, b, preferred_element_type=F32)


def _dot_nt(a, b):
    return lax.dot_general(a, b, (((1,), (1,)), ((), ())), preferred_element_type=F32)


def _head(h):
    return slice(h * HEAD_DIM, (h + 1) * HEAD_DIM)


def _lane_chunk(c):
    return slice(c * LANES, (c + 1) * LANES)


def _tile_scale(scaled_tiles):
    if scaled_tiles is None or not any(scaled_tiles):
        return None
    if all(scaled_tiles):
        return SCALE2
    j = pl.program_id(1)
    is_scaled = functools.reduce(jnp.logical_or, [j == t for t, on in enumerate(scaled_tiles) if on])
    return jnp.where(is_scaled, SCALE2, 1.0)


def _mm_kernel(x_ref, w_ref, o_ref, *rest, scaled_tiles, transposed_tile, cast_rows):
    rest = list(rest)
    vt_ref = rest.pop(0) if transposed_tile is not None else None
    if cast_rows:
        xb_ref = rest.pop(0)

        @pl.when(pl.program_id(1) == 0)
        def _():
            xb_ref[...] = x_ref[...].astype(BF16)

        rows = xb_ref[...]
    else:
        rows = x_ref[...]
    acc = _dot(rows, w_ref[...].astype(BF16))
    scale = _tile_scale(scaled_tiles)
    o_ref[...] = (acc if scale is None else acc * scale).astype(o_ref.dtype)
    if vt_ref is not None:
        @pl.when(pl.program_id(1) == transposed_tile)
        def _():
            vt_ref[0] = acc.T.astype(vt_ref.dtype)


def _matmul(x, w, layer, tm, tn, name, query_cols=(), transposed_cols=None, seq_len=None):
    m, k = x.shape
    n = w.shape[2]
    is_f32 = x.dtype == F32
    scaled_tiles = tuple(any(lo <= j * tn and (j + 1) * tn <= hi for lo, hi in query_cols)
                         for j in range(n // tn))
    assert sum(scaled_tiles) * tn == sum(hi - lo for lo, hi in query_cols)
    out_specs = [pl.BlockSpec((tm, tn), lambda i, j: (i, j))]
    out_shape = [jax.ShapeDtypeStruct((m, n), BF16)]
    transposed_tile = None
    if transposed_cols is not None:
        lo, hi = transposed_cols
        assert hi - lo == tn and lo % tn == 0 and seq_len % tm == 0
        transposed_tile = lo // tn
        per_seq = seq_len // tm
        out_specs.append(pl.BlockSpec((1, tn, tm), lambda i, j: (i // per_seq, 0, i % per_seq)))
        out_shape.append(jax.ShapeDtypeStruct((m // seq_len, tn, seq_len), BF16))
    body = functools.partial(_mm_kernel, scaled_tiles=scaled_tiles, transposed_tile=transposed_tile,
                             cast_rows=is_f32)
    outs = pl.pallas_call(
        body,
        grid=(m // tm, n // tn),
        in_specs=[pl.BlockSpec((tm, k), lambda i, j: (i, 0)),
                  pl.BlockSpec((None, k, tn), lambda i, j: (layer, 0, j))],
        out_specs=out_specs,
        out_shape=out_shape,
        scratch_shapes=[pltpu.VMEM((tm, k), BF16)] if is_f32 else [],
        compiler_params=_params(("parallel", "arbitrary"), VMEM_LIMIT_BIG_TILES),
        name=name,
    )(x, w)
    return outs if transposed_cols is not None else outs[0]


def _matmul_all_layers(x, w, tn, name):
    m, k = x.shape
    nl, _, n = w.shape
    return pl.pallas_call(
        functools.partial(_mm_kernel, scaled_tiles=None, transposed_tile=None, cast_rows=True),
        grid=(nl, n // tn),
        in_specs=[pl.BlockSpec((m, k), lambda l, j: (0, 0)),
                  pl.BlockSpec((None, k, tn), lambda l, j: (l, 0, j))],
        out_specs=pl.BlockSpec((None, m, tn), lambda l, j: (l, 0, j)),
        out_shape=jax.ShapeDtypeStruct((nl, m, n), BF16),
        scratch_shapes=[pltpu.VMEM((m, k), BF16)],
        compiler_params=_params(("arbitrary", "arbitrary")),
        name=name,
    )(x, w)


def _kv_kernel(x_ref, w_ref, kv_ref, km_ref):
    acc = _dot(x_ref[...].astype(BF16), w_ref[...].astype(BF16))
    kv_ref[...] = acc.astype(kv_ref.dtype)
    nblk = acc.shape[0] // MOBA_BLOCK
    k = acc[:, :KV_W].reshape(nblk, MOBA_BLOCK, KV_W)
    km_ref[...] = jnp.mean(k, axis=1)[:, None, :]


def _shared_kv(x, w, tm):
    m, k = x.shape
    n = w.shape[1]
    nblk = tm // MOBA_BLOCK
    return pl.pallas_call(
        _kv_kernel,
        grid=(m // tm,),
        in_specs=[pl.BlockSpec((tm, k), lambda i: (i, 0)),
                  pl.BlockSpec((k, n), lambda i: (0, 0))],
        out_specs=[pl.BlockSpec((tm, n), lambda i: (i, 0)),
                   pl.BlockSpec((nblk, 1, KV_W), lambda i: (i, 0, 0))],
        out_shape=[jax.ShapeDtypeStruct((m, n), BF16),
                   jax.ShapeDtypeStruct((m // MOBA_BLOCK, 1, KV_W), F32)],
        compiler_params=_params(("parallel",)),
        name="shared_kv",
    )(x, w)


def _layer_norm(r, g, b):
    mu = jnp.mean(r, axis=-1, keepdims=True)
    c = r - mu
    var = jnp.mean(c * c, axis=-1, keepdims=True)
    return c * lax.rsqrt(var + LN_EPS) * g + b


def _memory_heads(qm_ref, mk_ref, mvt_ref, o_ref):
    tq = qm_ref.shape[1]
    step = min(tq, 2 * LANES)
    units = [(h, r0) for r0 in range(0, tq, step) for h in range(N_MEM_HEADS)]
    scores = [_dot_nt(mk_ref[0, :, _head(h)], qm_ref[0, r0:r0 + step, _head(h)])
              for h, r0 in units]
    probs, dens = [], []
    for s in scores:
        m = jnp.max(s, axis=0, keepdims=True)
        p = jnp.exp2(s - m)
        dens.append(jnp.sum(p, axis=0, keepdims=True))
        probs.append(p.astype(BF16))
    outs = [_dot(mvt_ref[0, _head(h), :], p) for (h, _), p in zip(units, probs)]
    for (h, r0), ot, l in zip(units, outs, dens):
        o_ref[0, r0:r0 + step, Q_W + h * HEAD_DIM:Q_W + (h + 1) * HEAD_DIM] = (ot / l).T.astype(o_ref.dtype)


def _swa_kernel(sink_ref, q_ref, kp_ref, kc_ref, vtp_ref, vtc_ref, qm_ref, mk_ref, mvt_ref, o_ref):
    tq = q_ref.shape[1]
    n = pl.program_id(1)
    sw = WINDOW
    c_io = lax.broadcasted_iota(jnp.int32, (2 * sw, sw), 0)
    r_io = lax.broadcasted_iota(jnp.int32, (2 * sw, sw), 1)
    dist_i = r_io + sw - c_io
    band = (dist_i >= 0) & (dist_i < sw)
    neg_dist = jnp.where(band, -dist_i.astype(F32), NEG_INF)
    first_row = jnp.where(n > 0, 0, sw)
    neg_dist_first = jnp.where(c_io >= first_row, neg_dist, NEG_INF)
    for sb in range(tq // sw):
        rows = slice(sb * sw, (sb + 1) * sw)
        nd = neg_dist_first if sb == 0 else neg_dist
        scores, vtws = [], []
        for kvh in range(N_KV_HEADS):
            cs = _head(kvh)
            if sb == 0:
                kw = jnp.concatenate([kp_ref[0, :, cs], kc_ref[0, :sw, cs]], axis=0)
                vtw = jnp.concatenate([vtp_ref[0, cs, :], vtc_ref[0, cs, :sw]], axis=1)
            else:
                kw = kc_ref[0, (sb - 1) * sw:(sb + 1) * sw, cs]
                vtw = vtc_ref[0, cs, (sb - 1) * sw:(sb + 1) * sw]
            q = jnp.concatenate([q_ref[0, rows, _head(kvh * GROUP + g)] for g in range(GROUP)], axis=0)
            scores.append(_dot_nt(kw, q))
            vtws.append(vtw)
        probs, denoms = [], []
        for h in range(N_Q_HEADS):
            kvh, g = divmod(h, GROUP)
            s = scores[kvh][:, g * sw:(g + 1) * sw] + SLOPES2[h] * nd
            sink = sink_ref[h] * LOG2E
            m = jnp.maximum(jnp.max(s, axis=0, keepdims=True), sink)
            p = jnp.exp2(s - m)
            denoms.append(jnp.sum(p, axis=0, keepdims=True) + jnp.exp2(sink - m))
            probs.append(p.astype(BF16))
        outs = [_dot(vtws[kvh], jnp.concatenate(probs[kvh * GROUP:(kvh + 1) * GROUP], axis=1))
                for kvh in range(N_KV_HEADS)]
        for h in range(N_Q_HEADS):
            kvh, g = divmod(h, GROUP)
            ot = outs[kvh][:, g * sw:(g + 1) * sw] / denoms[h]
            o_ref[0, rows, _head(h)] = ot.T.astype(o_ref.dtype)
    _memory_heads(qm_ref, mk_ref, mvt_ref, o_ref)


def _swa_layer(proj, vt, mkv, mvt, sinks, tq):
    b, t, _ = proj.shape
    mlen = mkv.shape[1]
    wpb = tq // WINDOW
    kcol = Q_W // KV_W
    prev = lambda n: jnp.maximum(n * wpb - 1, 0)
    return pl.pallas_call(
        _swa_kernel,
        grid=(b, t // tq),
        in_specs=[
            pl.BlockSpec(memory_space=pltpu.SMEM),
            pl.BlockSpec((1, tq, Q_W), lambda bi, n: (bi, n, 0)),
            pl.BlockSpec((1, WINDOW, KV_W), lambda bi, n: (bi, prev(n), kcol)),
            pl.BlockSpec((1, tq, KV_W), lambda bi, n: (bi, n, kcol)),
            pl.BlockSpec((1, KV_W, WINDOW), lambda bi, n: (bi, 0, prev(n))),
            pl.BlockSpec((1, KV_W, tq), lambda bi, n: (bi, 0, n)),
            pl.BlockSpec((1, tq, MQ_W), lambda bi, n: (bi, n, kcol + 2)),
            pl.BlockSpec((1, mlen, MQ_W), lambda bi, n: (bi, 0, 0)),
            pl.BlockSpec((1, MQ_W, mlen), lambda bi, n: (bi, 0, 0)),
        ],
        out_specs=pl.BlockSpec((1, tq, MIX_W), lambda bi, n: (bi, n, 0)),
        out_shape=jax.ShapeDtypeStruct((b, t, MIX_W), BF16),
        compiler_params=_params(("parallel", "arbitrary")),
        name="swa_mem_attn",
    )(sinks, proj, proj, proj, vt, vt, proj, mkv, mvt)


def _moba_kernel(q_ref, k_ref, vt_ref, km_ref, qm_ref, mk_ref, mvt_ref, o_ref,
                 al_ref, sel_ref, m_ref, l_ref, acc_ref):
    tq = MOBA_BLOCK
    nb = k_ref.shape[1]
    per_head = tq // LANES
    j = pl.program_id(1)
    key = lax.broadcasted_iota(jnp.int32, (tq, tq), 0)
    qry = lax.broadcasted_iota(jnp.int32, (tq, tq), 1)
    s_minus_t = (key - qry).astype(F32)
    causal_add = jnp.where(key <= qry, 0.0, NEG_INF)
    blk = lax.broadcasted_iota(jnp.int32, (nb, tq), 0)
    heads = range(N_Q_HEADS)

    @pl.when((pl.program_id(0) == 0) & (j == 0))
    def _():
        for h in heads:
            al_ref[h] = SLOPES2[h] * s_minus_t

    def head_lanes(h):
        return slice(h * tq, (h + 1) * tq)

    def scores_of(i):
        return [_dot_nt(k_ref[0, i, :, _head(h // GROUP)], q_ref[0, :, _head(h)]) for h in heads]

    def values_of(i, probs):
        return [_dot(vt_ref[0, i, _head(h // GROUP), :], probs[h]) for h in heads]

    for h in heads:
        kmb = km_ref[0, :, _head(h // GROUP)].astype(BF16)
        gate = jnp.where(blk < j, _dot_nt(kmb, q_ref[0, :, _head(h)]), NEG_INF)
        cnt = jnp.zeros((nb, tq), F32)
        for ip in range(nb):
            gi = gate[ip:ip + 1, :]
            beats = (gi > gate) | ((gi == gate) & (ip < blk))
            cnt = cnt + jnp.where(beats, 1.0, 0.0)
        sel = jnp.where((cnt < MOBA_TOPK) & (blk < j), 0.0, NEG_INF)
        for ip in range(nb):
            sel_ref[ip, :, head_lanes(h)] = sel[ip:ip + 1, :]
    slope_row = jnp.concatenate([jnp.full((1, tq), SLOPES2[h], F32) for h in heads], axis=1)

    raw = scores_of(j)
    probs = []
    for h in heads:
        ps = []
        for half in range(per_head):
            lanes, hl = _lane_chunk(h * per_head + half), _lane_chunk(half)
            s = raw[h][:, hl] + (al_ref[h, :, hl] + causal_add[:, hl])
            m0 = jnp.max(s, axis=0, keepdims=True)
            p = jnp.exp2(s - m0)
            m_ref[:, lanes] = m0
            l_ref[:, lanes] = jnp.sum(p, axis=0, keepdims=True)
            ps.append(p.astype(BF16))
        probs.append(jnp.concatenate(ps, axis=1))
    for h, pv in zip(heads, values_of(j, probs)):
        acc_ref[:, head_lanes(h)] = pv

    def past_block(i, carry):
        gap = ((j - i) * tq).astype(F32)
        rowc = sel_ref[i] - slope_row * gap
        raw = scores_of(i)
        probs, scales = [], []
        for h in heads:
            ps, sc = [], []
            for half in range(per_head):
                lanes, hl = _lane_chunk(h * per_head + half), _lane_chunk(half)
                s = raw[h][:, hl] + al_ref[h, :, hl]
                m_old = m_ref[:, lanes]
                m_new = jnp.maximum(m_old, jnp.max(s, axis=0, keepdims=True) + rowc[:, lanes])
                a = jnp.exp2(m_old - m_new)
                p = jnp.exp2(s - (m_new - rowc[:, lanes]))
                l_ref[:, lanes] = a * l_ref[:, lanes] + jnp.sum(p, axis=0, keepdims=True)
                m_ref[:, lanes] = m_new
                ps.append(p.astype(BF16))
                sc.append(a)
            probs.append(jnp.concatenate(ps, axis=1))
            scales.append(jnp.concatenate(sc, axis=1))
        for h, pv in zip(heads, values_of(i, probs)):
            acc_ref[:, head_lanes(h)] = scales[h] * acc_ref[:, head_lanes(h)] + pv
        return carry

    lax.fori_loop(0, j, past_block, 0)
    for h in heads:
        ot = acc_ref[:, head_lanes(h)] / l_ref[:, head_lanes(h)]
        o_ref[0, :, _head(h)] = ot.T.astype(o_ref.dtype)
    _memory_heads(qm_ref, mk_ref, mvt_ref, o_ref)


def _moba_layer(proj, k4, vt4, kmeans, mkv, mvt):
    b, t, _ = proj.shape
    mlen = mkv.shape[1]
    tq = MOBA_BLOCK
    nb = t // tq
    hq = N_Q_HEADS * tq
    return pl.pallas_call(
        _moba_kernel,
        grid=(b, nb),
        in_specs=[
            pl.BlockSpec((1, tq, Q_W), lambda bi, n: (bi, n, 0)),
            pl.BlockSpec((1, nb, tq, KV_W), lambda bi, n: (bi, 0, 0, 0)),
            pl.BlockSpec((1, nb, KV_W, tq), lambda bi, n: (bi, 0, 0, 0)),
            pl.BlockSpec((1, nb, KV_W), lambda bi, n: (bi, 0, 0)),
            pl.BlockSpec((1, tq, MQ_W), lambda bi, n: (bi, n, Q_W // MQ_W)),
            pl.BlockSpec((1, mlen, MQ_W), lambda bi, n: (bi, 0, 0)),
            pl.BlockSpec((1, MQ_W, mlen), lambda bi, n: (bi, 0, 0)),
        ],
        out_specs=pl.BlockSpec((1, tq, MIX_W), lambda bi, n: (bi, n, 0)),
        out_shape=jax.ShapeDtypeStruct((b, t, MIX_W), BF16),
        scratch_shapes=[pltpu.VMEM((N_Q_HEADS, tq, tq), F32),
                        pltpu.VMEM((nb, 1, hq), F32),
                        pltpu.VMEM((1, hq), F32),
                        pltpu.VMEM((1, hq), F32),
                        pltpu.VMEM((HEAD_DIM, hq), F32)],
        compiler_params=_params(("arbitrary", "arbitrary")),
        name="moba_mem_attn",
    )(proj, k4, vt4, kmeans, proj, mkv, mvt)


OUT_ROW_CHUNK = 256


def _out_proj_kernel(mix_ref, w_ref, h_ref, g_ref, b_ref, o_ref, wb_ref):
    @pl.when(pl.program_id(0) == 0)
    def _():
        wb_ref[...] = w_ref[...].astype(BF16)

    for r0 in range(0, o_ref.shape[0], OUT_ROW_CHUNK):
        rows = slice(r0, r0 + OUT_ROW_CHUNK)
        r = ALPHA * h_ref[rows, :] + _dot(mix_ref[rows, :], wb_ref[...])
        o_ref[rows, :] = _layer_norm(r, g_ref[...], b_ref[...])


def _ln_specs(ln_index):
    spec = pl.BlockSpec((None, 1, D_MODEL), lambda *_: (ln_index, 0, 0))
    return [spec, spec]


def _out_proj_ln(mix, w, layer, h, ln_g, ln_b, tm):
    m, k = mix.shape
    n = w.shape[2]
    return pl.pallas_call(
        _out_proj_kernel,
        grid=(m // tm,),
        in_specs=[pl.BlockSpec((tm, k), lambda i: (i, 0)),
                  pl.BlockSpec((None, k, n), lambda i: (layer, 0, 0), pipeline_mode=pl.Buffered(1)),
                  pl.BlockSpec((tm, n), lambda i: (i, 0))] + _ln_specs(2 * layer),
        out_specs=pl.BlockSpec((tm, n), lambda i: (i, 0)),
        out_shape=jax.ShapeDtypeStruct((m, n), F32),
        scratch_shapes=[pltpu.VMEM((k, n), BF16)],
        compiler_params=_params(("arbitrary",)),
        name="out_proj_ln",
    )(mix, w, h, ln_g, ln_b)


def _ffn_kernel(h_ref, wu_ref, wd_ref, g_ref, b_ref, o_ref, ob_ref):
    f = pl.program_id(1)

    @pl.when(f == 0)
    def _():
        ob_ref[...] = h_ref[...].astype(BF16)
        o_ref[...] = ALPHA * h_ref[...]

    u = jnp.maximum(_dot(ob_ref[...], wu_ref[...].astype(BF16)), 0.0)
    o_ref[...] += _dot((u * u).astype(BF16), wd_ref[...].astype(BF16))

    @pl.when(f == pl.num_programs(1) - 1)
    def _():
        y = _layer_norm(o_ref[...], g_ref[...], b_ref[...])
        o_ref[...] = y
        ob_ref[...] = y.astype(BF16)


def _ffn_ln(h, w_up, w_down, layer, ln_g, ln_b, tm, tf, emit_bf16):
    m, d = h.shape
    dff = w_up.shape[2]
    row_spec = pl.BlockSpec((tm, d), lambda i, f: (i, 0))
    outs = pl.pallas_call(
        _ffn_kernel,
        grid=(m // tm, dff // tf),
        in_specs=[row_spec,
                  pl.BlockSpec((None, d, tf), lambda i, f: (layer, 0, f)),
                  pl.BlockSpec((None, tf, d), lambda i, f: (layer, f, 0))] + _ln_specs(2 * layer + 1),
        out_specs=[row_spec, row_spec] if emit_bf16 else [row_spec],
        out_shape=[jax.ShapeDtypeStruct((m, d), F32)] + ([jax.ShapeDtypeStruct((m, d), BF16)] if emit_bf16 else []),
        scratch_shapes=[] if emit_bf16 else [pltpu.VMEM((tm, d), BF16)],
        compiler_params=_params(("parallel", "arbitrary"), VMEM_LIMIT_BIG_TILES),
        name="ffn_ln",
    )(h, w_up, w_down, ln_g, ln_b)
    return (outs[0], outs[1]) if emit_bf16 else (outs[0], None)


PROJ_TM_F32, PROJ_TM_BF16 = 2048, 2048
PROJ_A_TN, PROJ_B_TN = 512, 1024
KV_TM = 1024
OUT_TM = 512
FFN_TM, FFN_TF = 1024, 512
SWA_TQ = 1024


def kernel(x, mem, w_in_a, sinks_a, w_q_b, w_kv_shared, w_mem_kv, w_o, w_up, w_down, ln_g, ln_b):
    b, t, d = x.shape
    mlen = mem.shape[1]
    m = b * t
    nb = t // MOBA_BLOCK
    h = x.reshape(m, d)
    mem2 = mem.reshape(b * mlen, d)
    ln_g2 = ln_g.reshape(2 * DEPTH, 1, d)
    ln_b2 = ln_b.reshape(2 * DEPTH, 1, d)
    mkv_all = _matmul_all_layers(mem2, w_mem_kv, 512, "mem_kv").reshape(DEPTH, b, mlen, 2 * MQ_W)
    mvt_all = jnp.swapaxes(mkv_all[..., MQ_W:], 2, 3)
    k4 = vt4 = kmeans = None
    hx = h
    for layer in range(DEPTH):
        proj_tm = PROJ_TM_F32 if hx.dtype == F32 else PROJ_TM_BF16
        mkv, mvt = mkv_all[layer], mvt_all[layer]
        if layer < N_A_LAYERS:
            proj, vt = _matmul(hx, w_in_a, layer, proj_tm, PROJ_A_TN, "proj_a",
                               query_cols=((0, Q_W), (Q_W + 2 * KV_W, Q_W + 2 * KV_W + MQ_W)),
                               transposed_cols=(Q_W + KV_W, Q_W + 2 * KV_W), seq_len=t)
            proj = proj.reshape(b, t, -1)
            mix = _swa_layer(proj, vt, mkv, mvt, sinks_a[layer], SWA_TQ)
        else:
            if k4 is None:
                kv, km = _shared_kv(hx, w_kv_shared, KV_TM)
                k4 = kv.reshape(b, nb, MOBA_BLOCK, 2 * KV_W)
                vt4 = jnp.swapaxes(k4[..., KV_W:], 2, 3)
                kmeans = km.reshape(b, nb, KV_W)
            proj = _matmul(hx, w_q_b, layer - N_A_LAYERS, proj_tm, PROJ_B_TN, "proj_b", query_cols=((0, MIX_W),))
            mix = _moba_layer(proj.reshape(b, t, -1), k4, vt4, kmeans, mkv, mvt)
        h = _out_proj_ln(mix.reshape(m, MIX_W), w_o, layer, h, ln_g2, ln_b2, OUT_TM)
        h, hx = _ffn_ln(h, w_up, w_down, layer, ln_g2, ln_b2, FFN_TM, FFN_TF, emit_bf16=layer + 1 < DEPTH)
    return h.reshape(b, t, d)
```

```python
import functools
import math
import struct

import jax
import jax.numpy as jnp
from jax import lax
from jax.experimental import pallas as pl
from jax.experimental.pallas import tpu as pltpu

D_MODEL = 2048
DEPTH = 4
HEAD_DIM = 128
N_Q_HEADS = 12
N_KV_HEADS = 4
GROUP = N_Q_HEADS // N_KV_HEADS
N_MEM_HEADS = 4
WINDOW = 128
MOBA_BLOCK = 256
MOBA_TOPK = 3
D_FF = 4 * D_MODEL
N_A_LAYERS = DEPTH // 2
ALPHA = (2 * DEPTH) ** 0.25
LN_EPS = 1e-5
NEG_INF = -1e30
Q_W = N_Q_HEADS * HEAD_DIM
KV_W = N_KV_HEADS * HEAD_DIM
MQ_W = N_MEM_HEADS * HEAD_DIM
MIX_W = Q_W + MQ_W
SCALE = HEAD_DIM ** -0.5
SLOPES = tuple(2.0 ** (-8.0 * h / N_Q_HEADS) for h in range(1, N_Q_HEADS + 1))
LOG2E = math.log2(math.e)
SCALE2 = SCALE * LOG2E
SLOPES2 = tuple(s * LOG2E for s in SLOPES)

V7X_VMEM_BYTES = 64 * 1024 * 1024
VMEM_LIMIT = V7X_VMEM_BYTES - 8 * 1024 * 1024
LANES = 128

BF16 = jnp.bfloat16
F32 = jnp.float32


VMEM_LIMIT_BIG_TILES = V7X_VMEM_BYTES - 1024 * 1024


def _params(sem, vmem_limit=VMEM_LIMIT):
    return pltpu.CompilerParams(dimension_semantics=sem, vmem_limit_bytes=vmem_limit)


def _dot(a, b):
    return jnp.dot(a, b, preferred_element_type=F32)


def _dot_nt(a, b):
    return lax.dot_general(a, b, (((1,), (1,)), ((), ())), preferred_element_type=F32)


def _head(h):
    return slice(h * HEAD_DIM, (h + 1) * HEAD_DIM)


def _lane_chunk(c):
    return slice(c * LANES, (c + 1) * LANES)


def _round_bf16(x):
    bits = struct.unpack("<I", struct.pack("<f", x))[0]
    bits = (bits + 0x7FFF + ((bits >> 16) & 1)) & 0xFFFF0000
    return struct.unpack("<f", struct.pack("<I", bits))[0]


SLOPE_TERMS = 3


def _bf16_split(x):
    parts = []
    for _ in range(SLOPE_TERMS):
        parts.append(_round_bf16(x))
        x = x - parts[-1]
    return tuple(parts)


def _tile_scale(scaled_tiles):
    if scaled_tiles is None or not any(scaled_tiles):
        return None
    if all(scaled_tiles):
        return SCALE2
    j = pl.program_id(1)
    is_scaled = functools.reduce(jnp.logical_or, [j == t for t, on in enumerate(scaled_tiles) if on])
    return jnp.where(is_scaled, SCALE2, 1.0)


def _mm_kernel(x_ref, w_ref, o_ref, *rest, scaled_tiles, transposed_tile, cast_rows):
    rest = list(rest)
    vt_ref = rest.pop(0) if transposed_tile is not None else None
    if cast_rows:
        xb_ref = rest.pop(0)

        @pl.when(pl.program_id(1) == 0)
        def _():
            xb_ref[...] = x_ref[...].astype(BF16)

        rows = xb_ref[...]
    else:
        rows = x_ref[...]
    acc = _dot(rows, w_ref[...].astype(BF16))
    scale = _tile_scale(scaled_tiles)
    o_ref[...] = (acc if scale is None else acc * scale).astype(o_ref.dtype)
    if vt_ref is not None:
        @pl.when(pl.program_id(1) == transposed_tile)
        def _():
            vt_ref[0] = acc.T.astype(vt_ref.dtype)


def _matmul(x, w, layer, tm, tn, name, query_cols=(), transposed_cols=None, seq_len=None):
    m, k = x.shape
    n = w.shape[2]
    is_f32 = x.dtype == F32
    scaled_tiles = tuple(any(lo <= j * tn and (j + 1) * tn <= hi for lo, hi in query_cols)
                         for j in range(n // tn))
    assert sum(scaled_tiles) * tn == sum(hi - lo for lo, hi in query_cols)
    out_specs = [pl.BlockSpec((tm, tn), lambda i, j: (i, j))]
    out_shape = [jax.ShapeDtypeStruct((m, n), BF16)]
    transposed_tile = None
    if transposed_cols is not None:
        lo, hi = transposed_cols
        assert hi - lo == tn and lo % tn == 0 and seq_len % tm == 0
        transposed_tile = lo // tn
        per_seq = seq_len // tm
        out_specs.append(pl.BlockSpec((1, tn, tm), lambda i, j: (i // per_seq, 0, i % per_seq)))
        out_shape.append(jax.ShapeDtypeStruct((m // seq_len, tn, seq_len), BF16))
    body = functools.partial(_mm_kernel, scaled_tiles=scaled_tiles, transposed_tile=transposed_tile,
                             cast_rows=is_f32)
    outs = pl.pallas_call(
        body,
        grid=(m // tm, n // tn),
        in_specs=[pl.BlockSpec((tm, k), lambda i, j: (i, 0)),
                  pl.BlockSpec((None, k, tn), lambda i, j: (layer, 0, j))],
        out_specs=out_specs,
        out_shape=out_shape,
        scratch_shapes=[pltpu.VMEM((tm, k), BF16)] if is_f32 else [],
        compiler_params=_params(("parallel", "arbitrary"), VMEM_LIMIT_BIG_TILES),
        name=name,
    )(x, w)
    return outs if transposed_cols is not None else outs[0]


def _matmul_all_layers(x, w, tn, name):
    m, k = x.shape
    nl, _, n = w.shape
    return pl.pallas_call(
        functools.partial(_mm_kernel, scaled_tiles=None, transposed_tile=None, cast_rows=True),
        grid=(nl, n // tn),
        in_specs=[pl.BlockSpec((m, k), lambda l, j: (0, 0)),
                  pl.BlockSpec((None, k, tn), lambda l, j: (l, 0, j))],
        out_specs=pl.BlockSpec((None, m, tn), lambda l, j: (l, 0, j)),
        out_shape=jax.ShapeDtypeStruct((nl, m, n), BF16),
        scratch_shapes=[pltpu.VMEM((m, k), BF16)],
        compiler_params=_params(("arbitrary", "arbitrary")),
        name=name,
    )(x, w)


def _kv_kernel(x_ref, w_ref, kv_ref, km_ref):
    acc = _dot(x_ref[...].astype(BF16), w_ref[...].astype(BF16))
    kv_ref[...] = acc.astype(kv_ref.dtype)
    nblk = acc.shape[0] // MOBA_BLOCK
    k = acc[:, :KV_W].reshape(nblk, MOBA_BLOCK, KV_W)
    km_ref[...] = jnp.mean(k, axis=1)[:, None, :]


def _shared_kv(x, w, tm):
    m, k = x.shape
    n = w.shape[1]
    nblk = tm // MOBA_BLOCK
    return pl.pallas_call(
        _kv_kernel,
        grid=(m // tm,),
        in_specs=[pl.BlockSpec((tm, k), lambda i: (i, 0)),
                  pl.BlockSpec((k, n), lambda i: (0, 0))],
        out_specs=[pl.BlockSpec((tm, n), lambda i: (i, 0)),
                   pl.BlockSpec((nblk, 1, KV_W), lambda i: (i, 0, 0))],
        out_shape=[jax.ShapeDtypeStruct((m, n), BF16),
                   jax.ShapeDtypeStruct((m // MOBA_BLOCK, 1, KV_W), F32)],
        compiler_params=_params(("parallel",)),
        name="shared_kv",
    )(x, w)


def _layer_norm(r, g, b):
    mu = jnp.mean(r, axis=-1, keepdims=True)
    c = r - mu
    var = jnp.mean(c * c, axis=-1, keepdims=True)
    return c * lax.rsqrt(var + LN_EPS) * g + b


def _memory_heads(qm_ref, mk_ref, mvt_ref, o_ref):
    tq = qm_ref.shape[1]
    step = min(tq, 2 * LANES)
    units = [(h, r0) for r0 in range(0, tq, step) for h in range(N_MEM_HEADS)]
    scores = [_dot_nt(mk_ref[0, :, _head(h)], qm_ref[0, r0:r0 + step, _head(h)])
              for h, r0 in units]
    probs, dens = [], []
    for s in scores:
        m = jnp.max(s, axis=0, keepdims=True)
        p = jnp.exp2(s - m)
        dens.append(jnp.sum(p, axis=0, keepdims=True))
        probs.append(p.astype(BF16))
    outs = [_dot(mvt_ref[0, _head(h), :], p) for (h, _), p in zip(units, probs)]
    for (h, r0), ot, l in zip(units, outs, dens):
        o_ref[0, r0:r0 + step, Q_W + h * HEAD_DIM:Q_W + (h + 1) * HEAD_DIM] = (ot / l).T.astype(o_ref.dtype)


def _swa_kernel(sink_ref, q_ref, kp_ref, kc_ref, vtp_ref, vtc_ref, qm_ref, mk_ref, mvt_ref, o_ref):
    tq = q_ref.shape[1]
    n = pl.program_id(1)
    sw = WINDOW
    c_io = lax.broadcasted_iota(jnp.int32, (2 * sw, sw), 0)
    r_io = lax.broadcasted_iota(jnp.int32, (2 * sw, sw), 1)
    dist_i = r_io + sw - c_io
    band = (dist_i >= 0) & (dist_i < sw)
    neg_dist = jnp.where(band, -dist_i.astype(F32), NEG_INF)
    first_row = jnp.where(n > 0, 0, sw)
    neg_dist_first = jnp.where(c_io >= first_row, neg_dist, NEG_INF)
    for sb in range(tq // sw):
        rows = slice(sb * sw, (sb + 1) * sw)
        nd = neg_dist_first if sb == 0 else neg_dist
        scores, vtws = [], []
        for kvh in range(N_KV_HEADS):
            cs = _head(kvh)
            if sb == 0:
                kw = jnp.concatenate([kp_ref[0, :, cs], kc_ref[0, :sw, cs]], axis=0)
                vtw = jnp.concatenate([vtp_ref[0, cs, :], vtc_ref[0, cs, :sw]], axis=1)
            else:
                kw = kc_ref[0, (sb - 1) * sw:(sb + 1) * sw, cs]
                vtw = vtc_ref[0, cs, (sb - 1) * sw:(sb + 1) * sw]
            q = jnp.concatenate([q_ref[0, rows, _head(kvh * GROUP + g)] for g in range(GROUP)], axis=0)
            scores.append(_dot_nt(kw, q))
            vtws.append(vtw)
        probs, denoms = [], []
        for h in range(N_Q_HEADS):
            kvh, g = divmod(h, GROUP)
            s = scores[kvh][:, g * sw:(g + 1) * sw] + SLOPES2[h] * nd
            sink = sink_ref[h] * LOG2E
            m = jnp.maximum(jnp.max(s, axis=0, keepdims=True), sink)
            p = jnp.exp2(s - m)
            denoms.append(jnp.sum(p, axis=0, keepdims=True) + jnp.exp2(sink - m))
            probs.append(p.astype(BF16))
        outs = [_dot(vtws[kvh], jnp.concatenate(probs[kvh * GROUP:(kvh + 1) * GROUP], axis=1))
                for kvh in range(N_KV_HEADS)]
        for h in range(N_Q_HEADS):
            kvh, g = divmod(h, GROUP)
            ot = outs[kvh][:, g * sw:(g + 1) * sw] / denoms[h]
            o_ref[0, rows, _head(h)] = ot.T.astype(o_ref.dtype)
    _memory_heads(qm_ref, mk_ref, mvt_ref, o_ref)


def _swa_layer(proj, vt, mkv, mvt, sinks, tq):
    b, t, _ = proj.shape
    mlen = mkv.shape[1]
    wpb = tq // WINDOW
    kcol = Q_W // KV_W
    prev = lambda n: jnp.maximum(n * wpb - 1, 0)
    return pl.pallas_call(
        _swa_kernel,
        grid=(b, t // tq),
        in_specs=[
            pl.BlockSpec(memory_space=pltpu.SMEM),
            pl.BlockSpec((1, tq, Q_W), lambda bi, n: (bi, n, 0)),
            pl.BlockSpec((1, WINDOW, KV_W), lambda bi, n: (bi, prev(n), kcol)),
            pl.BlockSpec((1, tq, KV_W), lambda bi, n: (bi, n, kcol)),
            pl.BlockSpec((1, KV_W, WINDOW), lambda bi, n: (bi, 0, prev(n))),
            pl.BlockSpec((1, KV_W, tq), lambda bi, n: (bi, 0, n)),
            pl.BlockSpec((1, tq, MQ_W), lambda bi, n: (bi, n, kcol + 2)),
            pl.BlockSpec((1, mlen, MQ_W), lambda bi, n: (bi, 0, 0)),
            pl.BlockSpec((1, MQ_W, mlen), lambda bi, n: (bi, 0, 0)),
        ],
        out_specs=pl.BlockSpec((1, tq, MIX_W), lambda bi, n: (bi, n, 0)),
        out_shape=jax.ShapeDtypeStruct((b, t, MIX_W), BF16),
        compiler_params=_params(("parallel", "arbitrary")),
        name="swa_mem_attn",
    )(sinks, proj, proj, proj, vt, vt, proj, mkv, mvt)


def _moba_kernel(q_ref, k_ref, vt_ref, km_ref, qm_ref, mk_ref, mvt_ref, o_ref,
                 qa_ref, sel_ref, m_ref, l_ref, acc_ref):
    tq = MOBA_BLOCK
    nb = k_ref.shape[1]
    per_head = tq // LANES
    j = pl.program_id(1)
    key = lax.broadcasted_iota(jnp.int32, (tq, tq), 0)
    qry = lax.broadcasted_iota(jnp.int32, (tq, tq), 1)
    causal_add = jnp.where(key <= qry, 0.0, NEG_INF)
    blk = lax.broadcasted_iota(jnp.int32, (nb, tq), 0)
    heads = range(N_Q_HEADS)

    lane_hd = lax.broadcasted_iota(jnp.int32, (tq, HEAD_DIM), 1)
    c_cols = jnp.where(lane_hd < SLOPE_TERMS, lax.broadcasted_iota(jnp.int32, (tq, HEAD_DIM), 0), 0).astype(BF16)
    for h in heads:
        slope_cols = sum(jnp.where(lane_hd == t, p, 0.0) for t, p in enumerate(_bf16_split(SLOPES2[h])))
        qa_ref[h] = jnp.concatenate([q_ref[0, :, _head(h)], slope_cols.astype(BF16)], axis=1)

    def head_lanes(h):
        return slice(h * tq, (h + 1) * tq)

    def scores_of(i):
        k_aug = [jnp.concatenate([k_ref[0, i, :, _head(kvh)], c_cols], axis=1) for kvh in range(N_KV_HEADS)]
        return [_dot_nt(k_aug[h // GROUP], qa_ref[h]) for h in heads]

    def values_of(i, probs):
        return [_dot(vt_ref[0, i, _head(h // GROUP), :], probs[h]) for h in heads]

    for h in heads:
        kmb = km_ref[0, :, _head(h // GROUP)].astype(BF16)
        gate = jnp.where(blk < j, _dot_nt(kmb, q_ref[0, :, _head(h)]), NEG_INF)
        cnt = jnp.zeros((nb, tq), F32)
        for ip in range(nb):
            gi = gate[ip:ip + 1, :]
            beats = (gi > gate) | ((gi == gate) & (ip < blk))
            cnt = cnt + jnp.where(beats, 1.0, 0.0)
        sel = jnp.where((cnt < MOBA_TOPK) & (blk < j), 0.0, NEG_INF)
        for ip in range(nb):
            sel_ref[ip, :, head_lanes(h)] = sel[ip:ip + 1, :]
    slope_row = jnp.concatenate([jnp.full((1, tq), SLOPES2[h], F32) for h in heads], axis=1)

    raw = scores_of(j)
    probs = []
    for h in heads:
        ps = []
        for half in range(per_head):
            lanes, hl = _lane_chunk(h * per_head + half), _lane_chunk(half)
            s = raw[h][:, hl] + causal_add[:, hl]
            m0 = jnp.max(s, axis=0, keepdims=True)
            p = jnp.exp2(s - m0)
            m_ref[:, lanes] = m0
            l_ref[:, lanes] = jnp.sum(p, axis=0, keepdims=True)
            ps.append(p.astype(BF16))
        probs.append(jnp.concatenate(ps, axis=1))
    for h, pv in zip(heads, values_of(j, probs)):
        acc_ref[:, head_lanes(h)] = pv

    def past_block(i, carry):
        gap = ((j - i) * tq).astype(F32)
        rowc = sel_ref[i] - slope_row * gap
        raw = scores_of(i)
        probs, scales = [], []
        for h in heads:
            ps, sc = [], []
            for half in range(per_head):
                lanes, hl = _lane_chunk(h * per_head + half), _lane_chunk(half)
                s = raw[h][:, hl]
                m_old = m_ref[:, lanes]
                m_new = jnp.maximum(m_old, jnp.max(s, axis=0, keepdims=True) + rowc[:, lanes])
                a = jnp.exp2(m_old - m_new)
                p = jnp.exp2(s - (m_new - rowc[:, lanes]))
                l_ref[:, lanes] = a * l_ref[:, lanes] + jnp.sum(p, axis=0, keepdims=True)
                m_ref[:, lanes] = m_new
                ps.append(p.astype(BF16))
                sc.append(a)
            probs.append(jnp.concatenate(ps, axis=1))
            scales.append(jnp.concatenate(sc, axis=1))
        for h, pv in zip(heads, values_of(i, probs)):
            acc_ref[:, head_lanes(h)] = scales[h] * acc_ref[:, head_lanes(h)] + pv
        return carry

    lax.fori_loop(0, j, past_block, 0)
    for h in heads:
        ot = acc_ref[:, head_lanes(h)] / l_ref[:, head_lanes(h)]
        o_ref[0, :, _head(h)] = ot.T.astype(o_ref.dtype)
    _memory_heads(qm_ref, mk_ref, mvt_ref, o_ref)


def _moba_layer(proj, k4, vt4, kmeans, mkv, mvt):
    b, t, _ = proj.shape
    mlen = mkv.shape[1]
    tq = MOBA_BLOCK
    nb = t // tq
    hq = N_Q_HEADS * tq
    return pl.pallas_call(
        _moba_kernel,
        grid=(b, nb),
        in_specs=[
            pl.BlockSpec((1, tq, Q_W), lambda bi, n: (bi, n, 0)),
            pl.BlockSpec((1, nb, tq, KV_W), lambda bi, n: (bi, 0, 0, 0)),
            pl.BlockSpec((1, nb, KV_W, tq), lambda bi, n: (bi, 0, 0, 0)),
            pl.BlockSpec((1, nb, KV_W), lambda bi, n: (bi, 0, 0)),
            pl.BlockSpec((1, tq, MQ_W), lambda bi, n: (bi, n, Q_W // MQ_W)),
            pl.BlockSpec((1, mlen, MQ_W), lambda bi, n: (bi, 0, 0)),
            pl.BlockSpec((1, MQ_W, mlen), lambda bi, n: (bi, 0, 0)),
        ],
        out_specs=pl.BlockSpec((1, tq, MIX_W), lambda bi, n: (bi, n, 0)),
        out_shape=jax.ShapeDtypeStruct((b, t, MIX_W), BF16),
        scratch_shapes=[pltpu.VMEM((N_Q_HEADS, tq, 2 * HEAD_DIM), BF16),
                        pltpu.VMEM((nb, 1, hq), F32),
                        pltpu.VMEM((1, hq), F32),
                        pltpu.VMEM((1, hq), F32),
                        pltpu.VMEM((HEAD_DIM, hq), F32)],
        compiler_params=_params(("parallel", "arbitrary")),
        name="moba_mem_attn",
    )(proj, k4, vt4, kmeans, proj, mkv, mvt)


OUT_ROW_CHUNK = 256


def _out_proj_kernel(mix_ref, w_ref, h_ref, g_ref, b_ref, o_ref, wb_ref):
    @pl.when(pl.program_id(0) == 0)
    def _():
        wb_ref[...] = w_ref[...].astype(BF16)

    for r0 in range(0, o_ref.shape[0], OUT_ROW_CHUNK):
        rows = slice(r0, r0 + OUT_ROW_CHUNK)
        r = ALPHA * h_ref[rows, :] + _dot(mix_ref[rows, :], wb_ref[...])
        o_ref[rows, :] = _layer_norm(r, g_ref[...], b_ref[...])


def _ln_specs(ln_index):
    spec = pl.BlockSpec((None, 1, D_MODEL), lambda *_: (ln_index, 0, 0))
    return [spec, spec]


def _out_proj_ln(mix, w, layer, h, ln_g, ln_b, tm):
    m, k = mix.shape
    n = w.shape[2]
    return pl.pallas_call(
        _out_proj_kernel,
        grid=(m // tm,),
        in_specs=[pl.BlockSpec((tm, k), lambda i: (i, 0)),
                  pl.BlockSpec((None, k, n), lambda i: (layer, 0, 0), pipeline_mode=pl.Buffered(1)),
                  pl.BlockSpec((tm, n), lambda i: (i, 0))] + _ln_specs(2 * layer),
        out_specs=pl.BlockSpec((tm, n), lambda i: (i, 0)),
        out_shape=jax.ShapeDtypeStruct((m, n), F32),
        scratch_shapes=[pltpu.VMEM((k, n), BF16)],
        compiler_params=_params(("arbitrary",)),
        name="out_proj_ln",
    )(mix, w, h, ln_g, ln_b)


def _ffn_kernel(h_ref, wu_ref, wd_ref, g_ref, b_ref, o_ref, ob_ref):
    f = pl.program_id(1)

    @pl.when(f == 0)
    def _():
        ob_ref[...] = h_ref[...].astype(BF16)
        o_ref[...] = ALPHA * h_ref[...]

    u = jnp.maximum(_dot(ob_ref[...], wu_ref[...].astype(BF16)), 0.0)
    o_ref[...] += _dot((u * u).astype(BF16), wd_ref[...].astype(BF16))

    @pl.when(f == pl.num_programs(1) - 1)
    def _():
        y = _layer_norm(o_ref[...], g_ref[...], b_ref[...])
        o_ref[...] = y
        ob_ref[...] = y.astype(BF16)


def _ffn_ln(h, w_up, w_down, layer, ln_g, ln_b, tm, tf, emit_bf16):
    m, d = h.shape
    dff = w_up.shape[2]
    row_spec = pl.BlockSpec((tm, d), lambda i, f: (i, 0))
    outs = pl.pallas_call(
        _ffn_kernel,
        grid=(m // tm, dff // tf),
        in_specs=[row_spec,
                  pl.BlockSpec((None, d, tf), lambda i, f: (layer, 0, f)),
                  pl.BlockSpec((None, tf, d), lambda i, f: (layer, f, 0))] + _ln_specs(2 * layer + 1),
        out_specs=[row_spec, row_spec] if emit_bf16 else [row_spec],
        out_shape=[jax.ShapeDtypeStruct((m, d), F32)] + ([jax.ShapeDtypeStruct((m, d), BF16)] if emit_bf16 else []),
        scratch_shapes=[] if emit_bf16 else [pltpu.VMEM((tm, d), BF16)],
        compiler_params=_params(("parallel", "arbitrary"), VMEM_LIMIT_BIG_TILES),
        name="ffn_ln",
    )(h, w_up, w_down, ln_g, ln_b)
    return (outs[0], outs[1]) if emit_bf16 else (outs[0], None)


PROJ_TM_F32, PROJ_TM_BF16 = 2048, 2048
PROJ_A_TN, PROJ_B_TN = 512, 1024
KV_TM = 1024
OUT_TM = 512
FFN_TM, FFN_TF = 1024, 512
SWA_TQ = 1024


def kernel(x, mem, w_in_a, sinks_a, w_q_b, w_kv_shared, w_mem_kv, w_o, w_up, w_down, ln_g, ln_b):
    b, t, d = x.shape
    mlen = mem.shape[1]
    m = b * t
    nb = t // MOBA_BLOCK
    h = x.reshape(m, d)
    mem2 = mem.reshape(b * mlen, d)
    ln_g2 = ln_g.reshape(2 * DEPTH, 1, d)
    ln_b2 = ln_b.reshape(2 * DEPTH, 1, d)
    mkv_all = _matmul_all_layers(mem2, w_mem_kv, 512, "mem_kv").reshape(DEPTH, b, mlen, 2 * MQ_W)
    mvt_all = jnp.swapaxes(mkv_all[..., MQ_W:], 2, 3)
    k4 = vt4 = kmeans = None
    hx = h
    for layer in range(DEPTH):
        proj_tm = PROJ_TM_F32 if hx.dtype == F32 else PROJ_TM_BF16
        mkv, mvt = mkv_all[layer], mvt_all[layer]
        if layer < N_A_LAYERS:
            proj, vt = _matmul(hx, w_in_a, layer, proj_tm, PROJ_A_TN, "proj_a",
                               query_cols=((0, Q_W), (Q_W + 2 * KV_W, Q_W + 2 * KV_W + MQ_W)),
                               transposed_cols=(Q_W + KV_W, Q_W + 2 * KV_W), seq_len=t)
            proj = proj.reshape(b, t, -1)
            mix = _swa_layer(proj, vt, mkv, mvt, sinks_a[layer], SWA_TQ)
        else:
            if k4 is None:
                kv, km = _shared_kv(hx, w_kv_shared, KV_TM)
                k4 = kv.reshape(b, nb, MOBA_BLOCK, 2 * KV_W)
                vt4 = jnp.swapaxes(k4[..., KV_W:], 2, 3)
                kmeans = km.reshape(b, nb, KV_W)
            proj = _matmul(hx, w_q_b, layer - N_A_LAYERS, proj_tm, PROJ_B_TN, "proj_b", query_cols=((0, MIX_W),))
            mix = _moba_layer(proj.reshape(b, t, -1), k4, vt4, kmeans, mkv, mvt)
        h = _out_proj_ln(mix.reshape(m, MIX_W), w_o, layer, h, ln_g2, ln_b2, OUT_TM)
        h, hx = _ffn_ln(h, w_up, w_down, layer, ln_g2, ln_b2, FFN_TM, FFN_TF, emit_bf16=layer + 1 < DEPTH)
    return h.reshape(b, t, d)
```

```python
import functools
import math
import struct

import jax
import jax.numpy as jnp
from jax import lax
from jax.experimental import pallas as pl
from jax.experimental.pallas import tpu as pltpu

D_MODEL = 2048
DEPTH = 4
HEAD_DIM = 128
N_Q_HEADS = 12
N_KV_HEADS = 4
GROUP = N_Q_HEADS // N_KV_HEADS
N_MEM_HEADS = 4
WINDOW = 128
MOBA_BLOCK = 256
MOBA_TOPK = 3
D_FF = 4 * D_MODEL
N_A_LAYERS = DEPTH // 2
ALPHA = (2 * DEPTH) ** 0.25
LN_EPS = 1e-5
NEG_INF = -1e30
Q_W = N_Q_HEADS * HEAD_DIM
KV_W = N_KV_HEADS * HEAD_DIM
MQ_W = N_MEM_HEADS * HEAD_DIM
MIX_W = Q_W + MQ_W
SCALE = HEAD_DIM ** -0.5
SLOPES = tuple(2.0 ** (-8.0 * h / N_Q_HEADS) for h in range(1, N_Q_HEADS + 1))
LOG2E = math.log2(math.e)
SCALE2 = SCALE * LOG2E
SLOPES2 = tuple(s * LOG2E for s in SLOPES)

V7X_VMEM_BYTES = 64 * 1024 * 1024
VMEM_LIMIT = V7X_VMEM_BYTES - 8 * 1024 * 1024
LANES = 128

BF16 = jnp.bfloat16
F32 = jnp.float32


VMEM_LIMIT_BIG_TILES = V7X_VMEM_BYTES - 1024 * 1024


def _params(sem, vmem_limit=VMEM_LIMIT):
    return pltpu.CompilerParams(dimension_semantics=sem, vmem_limit_bytes=vmem_limit)


def _dot(a, b):
    return jnp.dot(a, b, preferred_element_type=F32)


def _dot_nt(a, b):
    return lax.dot_general(a, b, (((1,), (1,)), ((), ())), preferred_element_type=F32)


def _head(h):
    return slice(h * HEAD_DIM, (h + 1) * HEAD_DIM)


def _lane_chunk(c):
    return slice(c * LANES, (c + 1) * LANES)


def _round_bf16(x):
    bits = struct.unpack("<I", struct.pack("<f", x))[0]
    bits = (bits + 0x7FFF + ((bits >> 16) & 1)) & 0xFFFF0000
    return struct.unpack("<f", struct.pack("<I", bits))[0]


SLOPE_TERMS = 3
DENOM_ROWS = 16


def _bf16_split(x):
    parts = []
    for _ in range(SLOPE_TERMS):
        parts.append(_round_bf16(x))
        x = x - parts[-1]
    return tuple(parts)


def _tile_scale(scaled_tiles):
    if scaled_tiles is None or not any(scaled_tiles):
        return None
    if all(scaled_tiles):
        return SCALE2
    j = pl.program_id(1)
    is_scaled = functools.reduce(jnp.logical_or, [j == t for t, on in enumerate(scaled_tiles) if on])
    return jnp.where(is_scaled, SCALE2, 1.0)


def _mm_kernel(x_ref, w_ref, o_ref, *rest, scaled_tiles, transposed_tile, cast_rows):
    rest = list(rest)
    vt_ref = rest.pop(0) if transposed_tile is not None else None
    if cast_rows:
        xb_ref = rest.pop(0)

        @pl.when(pl.program_id(1) == 0)
        def _():
            xb_ref[...] = x_ref[...].astype(BF16)

        rows = xb_ref[...]
    else:
        rows = x_ref[...]
    acc = _dot(rows, w_ref[...].astype(BF16))
    scale = _tile_scale(scaled_tiles)
    o_ref[...] = (acc if scale is None else acc * scale).astype(o_ref.dtype)
    if vt_ref is not None:
        @pl.when(pl.program_id(1) == transposed_tile)
        def _():
            vt_ref[0] = acc.T.astype(vt_ref.dtype)


def _matmul(x, w, layer, tm, tn, name, query_cols=(), transposed_cols=None, seq_len=None):
    m, k = x.shape
    n = w.shape[2]
    is_f32 = x.dtype == F32
    scaled_tiles = tuple(any(lo <= j * tn and (j + 1) * tn <= hi for lo, hi in query_cols)
                         for j in range(n // tn))
    assert sum(scaled_tiles) * tn == sum(hi - lo for lo, hi in query_cols)
    out_specs = [pl.BlockSpec((tm, tn), lambda i, j: (i, j))]
    out_shape = [jax.ShapeDtypeStruct((m, n), BF16)]
    transposed_tile = None
    if transposed_cols is not None:
        lo, hi = transposed_cols
        assert hi - lo == tn and lo % tn == 0 and seq_len % tm == 0
        transposed_tile = lo // tn
        per_seq = seq_len // tm
        out_specs.append(pl.BlockSpec((1, tn, tm), lambda i, j: (i // per_seq, 0, i % per_seq)))
        out_shape.append(jax.ShapeDtypeStruct((m // seq_len, tn, seq_len), BF16))
    body = functools.partial(_mm_kernel, scaled_tiles=scaled_tiles, transposed_tile=transposed_tile,
                             cast_rows=is_f32)
    outs = pl.pallas_call(
        body,
        grid=(m // tm, n // tn),
        in_specs=[pl.BlockSpec((tm, k), lambda i, j: (i, 0)),
                  pl.BlockSpec((None, k, tn), lambda i, j: (layer, 0, j))],
        out_specs=out_specs,
        out_shape=out_shape,
        scratch_shapes=[pltpu.VMEM((tm, k), BF16)] if is_f32 else [],
        compiler_params=_params(("parallel", "arbitrary"), VMEM_LIMIT_BIG_TILES),
        name=name,
    )(x, w)
    return outs if transposed_cols is not None else outs[0]


def _matmul_all_layers(x, w, tn, name):
    m, k = x.shape
    nl, _, n = w.shape
    return pl.pallas_call(
        functools.partial(_mm_kernel, scaled_tiles=None, transposed_tile=None, cast_rows=True),
        grid=(nl, n // tn),
        in_specs=[pl.BlockSpec((m, k), lambda l, j: (0, 0)),
                  pl.BlockSpec((None, k, tn), lambda l, j: (l, 0, j))],
        out_specs=pl.BlockSpec((None, m, tn), lambda l, j: (l, 0, j)),
        out_shape=jax.ShapeDtypeStruct((nl, m, n), BF16),
        scratch_shapes=[pltpu.VMEM((m, k), BF16)],
        compiler_params=_params(("arbitrary", "arbitrary")),
        name=name,
    )(x, w)


def _kv_kernel(x_ref, w_ref, kv_ref, km_ref):
    acc = _dot(x_ref[...].astype(BF16), w_ref[...].astype(BF16))
    kv_ref[...] = acc.astype(kv_ref.dtype)
    nblk = acc.shape[0] // MOBA_BLOCK
    k = acc[:, :KV_W].reshape(nblk, MOBA_BLOCK, KV_W)
    km_ref[...] = jnp.mean(k, axis=1)[:, None, :]


def _shared_kv(x, w, tm):
    m, k = x.shape
    n = w.shape[1]
    nblk = tm // MOBA_BLOCK
    return pl.pallas_call(
        _kv_kernel,
        grid=(m // tm,),
        in_specs=[pl.BlockSpec((tm, k), lambda i: (i, 0)),
                  pl.BlockSpec((k, n), lambda i: (0, 0))],
        out_specs=[pl.BlockSpec((tm, n), lambda i: (i, 0)),
                   pl.BlockSpec((nblk, 1, KV_W), lambda i: (i, 0, 0))],
        out_shape=[jax.ShapeDtypeStruct((m, n), BF16),
                   jax.ShapeDtypeStruct((m // MOBA_BLOCK, 1, KV_W), F32)],
        compiler_params=_params(("parallel",)),
        name="shared_kv",
    )(x, w)


def _layer_norm(r, g, b):
    mu = jnp.mean(r, axis=-1, keepdims=True)
    c = r - mu
    var = jnp.mean(c * c, axis=-1, keepdims=True)
    return c * lax.rsqrt(var + LN_EPS) * g + b


def _memory_heads(qm_ref, mk_ref, mvt_ref, o_ref):
    tq = qm_ref.shape[1]
    step = min(tq, 2 * LANES)
    units = [(h, r0) for r0 in range(0, tq, step) for h in range(N_MEM_HEADS)]
    scores = [_dot_nt(mk_ref[0, :, _head(h)], qm_ref[0, r0:r0 + step, _head(h)])
              for h, r0 in units]
    probs, dens = [], []
    for s in scores:
        m = jnp.max(s, axis=0, keepdims=True)
        p = jnp.exp2(s - m)
        dens.append(jnp.sum(p, axis=0, keepdims=True))
        probs.append(p.astype(BF16))
    outs = [_dot(mvt_ref[0, _head(h), :], p) for (h, _), p in zip(units, probs)]
    for (h, r0), ot, l in zip(units, outs, dens):
        o_ref[0, r0:r0 + step, Q_W + h * HEAD_DIM:Q_W + (h + 1) * HEAD_DIM] = (ot / l).T.astype(o_ref.dtype)


def _swa_kernel(sink_ref, q_ref, kp_ref, kc_ref, vtp_ref, vtc_ref, qm_ref, mk_ref, mvt_ref, o_ref):
    tq = q_ref.shape[1]
    n = pl.program_id(1)
    sw = WINDOW
    c_io = lax.broadcasted_iota(jnp.int32, (2 * sw, sw), 0)
    r_io = lax.broadcasted_iota(jnp.int32, (2 * sw, sw), 1)
    dist_i = r_io + sw - c_io
    band = (dist_i >= 0) & (dist_i < sw)
    neg_dist = jnp.where(band, -dist_i.astype(F32), NEG_INF)
    first_row = jnp.where(n > 0, 0, sw)
    neg_dist_first = jnp.where(c_io >= first_row, neg_dist, NEG_INF)
    for sb in range(tq // sw):
        rows = slice(sb * sw, (sb + 1) * sw)
        nd = neg_dist_first if sb == 0 else neg_dist
        scores, vtws = [], []
        for kvh in range(N_KV_HEADS):
            cs = _head(kvh)
            if sb == 0:
                kw = jnp.concatenate([kp_ref[0, :, cs], kc_ref[0, :sw, cs]], axis=0)
                vtw = jnp.concatenate([vtp_ref[0, cs, :], vtc_ref[0, cs, :sw]], axis=1)
            else:
                kw = kc_ref[0, (sb - 1) * sw:(sb + 1) * sw, cs]
                vtw = vtc_ref[0, cs, (sb - 1) * sw:(sb + 1) * sw]
            q = jnp.concatenate([q_ref[0, rows, _head(kvh * GROUP + g)] for g in range(GROUP)], axis=0)
            scores.append(_dot_nt(kw, q))
            vtws.append(vtw)
        probs, denoms = [], []
        for h in range(N_Q_HEADS):
            kvh, g = divmod(h, GROUP)
            s = scores[kvh][:, g * sw:(g + 1) * sw] + SLOPES2[h] * nd
            sink = sink_ref[h] * LOG2E
            m = jnp.maximum(jnp.max(s, axis=0, keepdims=True), sink)
            p = jnp.exp2(s - m)
            denoms.append(jnp.sum(p, axis=0, keepdims=True) + jnp.exp2(sink - m))
            probs.append(p.astype(BF16))
        outs = [_dot(vtws[kvh], jnp.concatenate(probs[kvh * GROUP:(kvh + 1) * GROUP], axis=1))
                for kvh in range(N_KV_HEADS)]
        for h in range(N_Q_HEADS):
            kvh, g = divmod(h, GROUP)
            ot = outs[kvh][:, g * sw:(g + 1) * sw] / denoms[h]
            o_ref[0, rows, _head(h)] = ot.T.astype(o_ref.dtype)
    _memory_heads(qm_ref, mk_ref, mvt_ref, o_ref)


def _swa_layer(proj, vt, mkv, mvt, sinks, tq):
    b, t, _ = proj.shape
    mlen = mkv.shape[1]
    wpb = tq // WINDOW
    kcol = Q_W // KV_W
    prev = lambda n: jnp.maximum(n * wpb - 1, 0)
    return pl.pallas_call(
        _swa_kernel,
        grid=(b, t // tq),
        in_specs=[
            pl.BlockSpec(memory_space=pltpu.SMEM),
            pl.BlockSpec((1, tq, Q_W), lambda bi, n: (bi, n, 0)),
            pl.BlockSpec((1, WINDOW, KV_W), lambda bi, n: (bi, prev(n), kcol)),
            pl.BlockSpec((1, tq, KV_W), lambda bi, n: (bi, n, kcol)),
            pl.BlockSpec((1, KV_W, WINDOW), lambda bi, n: (bi, 0, prev(n))),
            pl.BlockSpec((1, KV_W, tq), lambda bi, n: (bi, 0, n)),
            pl.BlockSpec((1, tq, MQ_W), lambda bi, n: (bi, n, kcol + 2)),
            pl.BlockSpec((1, mlen, MQ_W), lambda bi, n: (bi, 0, 0)),
            pl.BlockSpec((1, MQ_W, mlen), lambda bi, n: (bi, 0, 0)),
        ],
        out_specs=pl.BlockSpec((1, tq, MIX_W), lambda bi, n: (bi, n, 0)),
        out_shape=jax.ShapeDtypeStruct((b, t, MIX_W), BF16),
        compiler_params=_params(("parallel", "arbitrary")),
        name="swa_mem_attn",
    )(sinks, proj, proj, proj, vt, vt, proj, mkv, mvt)


def _moba_kernel(q_ref, k_ref, vt_ref, km_ref, qm_ref, mk_ref, mvt_ref, o_ref,
                 qa_ref, sel_ref, m_ref, acc_ref):
    tq = MOBA_BLOCK
    nb = k_ref.shape[1]
    per_head = tq // LANES
    j = pl.program_id(1)
    key = lax.broadcasted_iota(jnp.int32, (tq, tq), 0)
    qry = lax.broadcasted_iota(jnp.int32, (tq, tq), 1)
    causal_add = jnp.where(key <= qry, 0.0, NEG_INF)
    blk = lax.broadcasted_iota(jnp.int32, (nb, tq), 0)
    heads = range(N_Q_HEADS)

    lane_hd = lax.broadcasted_iota(jnp.int32, (tq, HEAD_DIM), 1)
    c_cols = jnp.where(lane_hd < SLOPE_TERMS, lax.broadcasted_iota(jnp.int32, (tq, HEAD_DIM), 0), 0).astype(BF16)
    for h in heads:
        slope_cols = sum(jnp.where(lane_hd == t, p, 0.0) for t, p in enumerate(_bf16_split(SLOPES2[h])))
        qa_ref[h] = jnp.concatenate([q_ref[0, :, _head(h)], slope_cols.astype(BF16)], axis=1)

    def head_lanes(h):
        return slice(h * tq, (h + 1) * tq)

    def scores_of(i):
        k_aug = [jnp.concatenate([k_ref[0, i, :, _head(kvh)], c_cols], axis=1) for kvh in range(N_KV_HEADS)]
        return [_dot_nt(k_aug[h // GROUP], qa_ref[h]) for h in heads]

    ones_rows = jnp.where(lax.broadcasted_iota(jnp.int32, (DENOM_ROWS, tq), 0) == 0, 1.0, 0.0).astype(BF16)

    def values_of(i, probs):
        vt_aug = [jnp.concatenate([vt_ref[0, i, _head(kvh), :], ones_rows], axis=0) for kvh in range(N_KV_HEADS)]
        return [_dot(vt_aug[h // GROUP], probs[h]) for h in heads]

    for h in heads:
        kmb = km_ref[0, :, _head(h // GROUP)].astype(BF16)
        gate = jnp.where(blk < j, _dot_nt(kmb, q_ref[0, :, _head(h)]), NEG_INF)
        cnt = jnp.zeros((nb, tq), F32)
        for ip in range(nb):
            gi = gate[ip:ip + 1, :]
            beats = (gi > gate) | ((gi == gate) & (ip < blk))
            cnt = cnt + jnp.where(beats, 1.0, 0.0)
        sel = jnp.where((cnt < MOBA_TOPK) & (blk < j), 0.0, NEG_INF)
        for ip in range(nb):
            sel_ref[ip, :, head_lanes(h)] = sel[ip:ip + 1, :]
    slope_row = jnp.concatenate([jnp.full((1, tq), SLOPES2[h], F32) for h in heads], axis=1)

    raw = scores_of(j)
    probs = []
    for h in heads:
        ps = []
        for half in range(per_head):
            lanes, hl = _lane_chunk(h * per_head + half), _lane_chunk(half)
            s = raw[h][:, hl] + causal_add[:, hl]
            m0 = jnp.max(s, axis=0, keepdims=True)
            p = jnp.exp2(s - m0)
            m_ref[:, lanes] = m0
            ps.append(p.astype(BF16))
        probs.append(jnp.concatenate(ps, axis=1))
    for h, pv in zip(heads, values_of(j, probs)):
        acc_ref[:, head_lanes(h)] = pv

    def past_block(i, carry):
        gap = ((j - i) * tq).astype(F32)
        rowc = sel_ref[i] - slope_row * gap
        raw = scores_of(i)
        probs, scales = [], []
        for h in heads:
            ps, sc = [], []
            for half in range(per_head):
                lanes, hl = _lane_chunk(h * per_head + half), _lane_chunk(half)
                s = raw[h][:, hl]
                m_old = m_ref[:, lanes]
                m_new = jnp.maximum(m_old, jnp.max(s, axis=0, keepdims=True) + rowc[:, lanes])
                a = jnp.exp2(m_old - m_new)
                p = jnp.exp2(s - (m_new - rowc[:, lanes]))
                m_ref[:, lanes] = m_new
                ps.append(p.astype(BF16))
                sc.append(a)
            probs.append(jnp.concatenate(ps, axis=1))
            scales.append(jnp.concatenate(sc, axis=1))
        for h, pv in zip(heads, values_of(i, probs)):
            acc_ref[:, head_lanes(h)] = scales[h] * acc_ref[:, head_lanes(h)] + pv
        return carry

    lax.fori_loop(0, j, past_block, 0)
    for h in heads:
        ot = acc_ref[:HEAD_DIM, head_lanes(h)] / acc_ref[HEAD_DIM:HEAD_DIM + 1, head_lanes(h)]
        o_ref[0, :, _head(h)] = ot.T.astype(o_ref.dtype)
    _memory_heads(qm_ref, mk_ref, mvt_ref, o_ref)


def _moba_layer(proj, k4, vt4, kmeans, mkv, mvt):
    b, t, _ = proj.shape
    mlen = mkv.shape[1]
    tq = MOBA_BLOCK
    nb = t // tq
    hq = N_Q_HEADS * tq
    return pl.pallas_call(
        _moba_kernel,
        grid=(b, nb),
        in_specs=[
            pl.BlockSpec((1, tq, Q_W), lambda bi, n: (bi, n, 0)),
            pl.BlockSpec((1, nb, tq, KV_W), lambda bi, n: (bi, 0, 0, 0)),
            pl.BlockSpec((1, nb, KV_W, tq), lambda bi, n: (bi, 0, 0, 0)),
            pl.BlockSpec((1, nb, KV_W), lambda bi, n: (bi, 0, 0)),
            pl.BlockSpec((1, tq, MQ_W), lambda bi, n: (bi, n, Q_W // MQ_W)),
            pl.BlockSpec((1, mlen, MQ_W), lambda bi, n: (bi, 0, 0)),
            pl.BlockSpec((1, MQ_W, mlen), lambda bi, n: (bi, 0, 0)),
        ],
        out_specs=pl.BlockSpec((1, tq, MIX_W), lambda bi, n: (bi, n, 0)),
        out_shape=jax.ShapeDtypeStruct((b, t, MIX_W), BF16),
        scratch_shapes=[pltpu.VMEM((N_Q_HEADS, tq, 2 * HEAD_DIM), BF16),
                        pltpu.VMEM((nb, 1, hq), F32),
                        pltpu.VMEM((1, hq), F32),
                        pltpu.VMEM((HEAD_DIM + DENOM_ROWS, hq), F32)],
        compiler_params=_params(("parallel", "arbitrary")),
        name="moba_mem_attn",
    )(proj, k4, vt4, kmeans, proj, mkv, mvt)


OUT_ROW_CHUNK = 256


def _out_proj_kernel(mix_ref, w_ref, h_ref, g_ref, b_ref, o_ref, wb_ref):
    @pl.when(pl.program_id(0) == 0)
    def _():
        wb_ref[...] = w_ref[...].astype(BF16)

    for r0 in range(0, o_ref.shape[0], OUT_ROW_CHUNK):
        rows = slice(r0, r0 + OUT_ROW_CHUNK)
        r = ALPHA * h_ref[rows, :] + _dot(mix_ref[rows, :], wb_ref[...])
        o_ref[rows, :] = _layer_norm(r, g_ref[...], b_ref[...])


def _ln_specs(ln_index):
    spec = pl.BlockSpec((None, 1, D_MODEL), lambda *_: (ln_index, 0, 0))
    return [spec, spec]


def _out_proj_ln(mix, w, layer, h, ln_g, ln_b, tm):
    m, k = mix.shape
    n = w.shape[2]
    return pl.pallas_call(
        _out_proj_kernel,
        grid=(m // tm,),
        in_specs=[pl.BlockSpec((tm, k), lambda i: (i, 0)),
                  pl.BlockSpec((None, k, n), lambda i: (layer, 0, 0), pipeline_mode=pl.Buffered(1)),
                  pl.BlockSpec((tm, n), lambda i: (i, 0))] + _ln_specs(2 * layer),
        out_specs=pl.BlockSpec((tm, n), lambda i: (i, 0)),
        out_shape=jax.ShapeDtypeStruct((m, n), F32),
        scratch_shapes=[pltpu.VMEM((k, n), BF16)],
        compiler_params=_params(("arbitrary",)),
        name="out_proj_ln",
    )(mix, w, h, ln_g, ln_b)


def _ffn_kernel(h_ref, wu_ref, wd_ref, g_ref, b_ref, o_ref, ob_ref):
    f = pl.program_id(1)

    @pl.when(f == 0)
    def _():
        ob_ref[...] = h_ref[...].astype(BF16)
        o_ref[...] = ALPHA * h_ref[...]

    u = jnp.maximum(_dot(ob_ref[...], wu_ref[...].astype(BF16)), 0.0)
    o_ref[...] += _dot((u * u).astype(BF16), wd_ref[...].astype(BF16))

    @pl.when(f == pl.num_programs(1) - 1)
    def _():
        y = _layer_norm(o_ref[...], g_ref[...], b_ref[...])
        o_ref[...] = y
        ob_ref[...] = y.astype(BF16)


def _ffn_ln(h, w_up, w_down, layer, ln_g, ln_b, tm, tf, emit_bf16):
    m, d = h.shape
    dff = w_up.shape[2]
    row_spec = pl.BlockSpec((tm, d), lambda i, f: (i, 0))
    outs = pl.pallas_call(
        _ffn_kernel,
        grid=(m // tm, dff // tf),
        in_specs=[row_spec,
                  pl.BlockSpec((None, d, tf), lambda i, f: (layer, 0, f)),
                  pl.BlockSpec((None, tf, d), lambda i, f: (layer, f, 0))] + _ln_specs(2 * layer + 1),
        out_specs=[row_spec, row_spec] if emit_bf16 else [row_spec],
        out_shape=[jax.ShapeDtypeStruct((m, d), F32)] + ([jax.ShapeDtypeStruct((m, d), BF16)] if emit_bf16 else []),
        scratch_shapes=[] if emit_bf16 else [pltpu.VMEM((tm, d), BF16)],
        compiler_params=_params(("parallel", "arbitrary"), VMEM_LIMIT_BIG_TILES),
        name="ffn_ln",
    )(h, w_up, w_down, ln_g, ln_b)
    return (outs[0], outs[1]) if emit_bf16 else (outs[0], None)


PROJ_TM_F32, PROJ_TM_BF16 = 2048, 2048
PROJ_A_TN, PROJ_B_TN = 512, 1024
KV_TM = 1024
OUT_TM = 512
FFN_TM, FFN_TF = 1024, 512
SWA_TQ = 1024


def kernel(x, mem, w_in_a, sinks_a, w_q_b, w_kv_shared, w_mem_kv, w_o, w_up, w_down, ln_g, ln_b):
    b, t, d = x.shape
    mlen = mem.shape[1]
    m = b * t
    nb = t // MOBA_BLOCK
    h = x.reshape(m, d)
    mem2 = mem.reshape(b * mlen, d)
    ln_g2 = ln_g.reshape(2 * DEPTH, 1, d)
    ln_b2 = ln_b.reshape(2 * DEPTH, 1, d)
    mkv_all = _matmul_all_layers(mem2, w_mem_kv, 512, "mem_kv").reshape(DEPTH, b, mlen, 2 * MQ_W)
    mvt_all = jnp.swapaxes(mkv_all[..., MQ_W:], 2, 3)
    k4 = vt4 = kmeans = None
    hx = h
    for layer in range(DEPTH):
        proj_tm = PROJ_TM_F32 if hx.dtype == F32 else PROJ_TM_BF16
        mkv, mvt = mkv_all[layer], mvt_all[layer]
        if layer < N_A_LAYERS:
            proj, vt = _matmul(hx, w_in_a, layer, proj_tm, PROJ_A_TN, "proj_a",
                               query_cols=((0, Q_W), (Q_W + 2 * KV_W, Q_W + 2 * KV_W + MQ_W)),
                               transposed_cols=(Q_W + KV_W, Q_W + 2 * KV_W), seq_len=t)
            proj = proj.reshape(b, t, -1)
            mix = _swa_layer(proj, vt, mkv, mvt, sinks_a[layer], SWA_TQ)
        else:
            if k4 is None:
                kv, km = _shared_kv(hx, w_kv_shared, KV_TM)
                k4 = kv.reshape(b, nb, MOBA_BLOCK, 2 * KV_W)
                vt4 = jnp.swapaxes(k4[..., KV_W:], 2, 3)
                kmeans = km.reshape(b, nb, KV_W)
            proj = _matmul(hx, w_q_b, layer - N_A_LAYERS, proj_tm, PROJ_B_TN, "proj_b", query_cols=((0, MIX_W),))
            mix = _moba_layer(proj.reshape(b, t, -1), k4, vt4, kmeans, mkv, mvt)
        h = _out_proj_ln(mix.reshape(m, MIX_W), w_o, layer, h, ln_g2, ln_b2, OUT_TM)
        h, hx = _ffn_ln(h, w_up, w_down, layer, ln_g2, ln_b2, FFN_TM, FFN_TF, emit_bf16=layer + 1 < DEPTH)
    return h.reshape(b, t, d)
```

```python
import functools
import math
import struct

import jax
import jax.numpy as jnp
from jax import lax
from jax.experimental import pallas as pl
from jax.experimental.pallas import tpu as pltpu

D_MODEL = 2048
DEPTH = 4
HEAD_DIM = 128
N_Q_HEADS = 12
N_KV_HEADS = 4
GROUP = N_Q_HEADS // N_KV_HEADS
N_MEM_HEADS = 4
WINDOW = 128
MOBA_BLOCK = 256
MOBA_TOPK = 3
D_FF = 4 * D_MODEL
N_A_LAYERS = DEPTH // 2
ALPHA = (2 * DEPTH) ** 0.25
LN_EPS = 1e-5
NEG_INF = -1e30
Q_W = N_Q_HEADS * HEAD_DIM
KV_W = N_KV_HEADS * HEAD_DIM
MQ_W = N_MEM_HEADS * HEAD_DIM
MIX_W = Q_W + MQ_W
SCALE = HEAD_DIM ** -0.5
SLOPES = tuple(2.0 ** (-8.0 * h / N_Q_HEADS) for h in range(1, N_Q_HEADS + 1))
LOG2E = math.log2(math.e)
SCALE2 = SCALE * LOG2E
SLOPES2 = tuple(s * LOG2E for s in SLOPES)

V7X_VMEM_BYTES = 64 * 1024 * 1024
VMEM_LIMIT = V7X_VMEM_BYTES - 8 * 1024 * 1024
LANES = 128

BF16 = jnp.bfloat16
F32 = jnp.float32


VMEM_LIMIT_BIG_TILES = V7X_VMEM_BYTES - 1024 * 1024


def _params(sem, vmem_limit=VMEM_LIMIT):
    return pltpu.CompilerParams(dimension_semantics=sem, vmem_limit_bytes=vmem_limit)


def _dot(a, b):
    return jnp.dot(a, b, preferred_element_type=F32)


def _dot_nt(a, b):
    return lax.dot_general(a, b, (((1,), (1,)), ((), ())), preferred_element_type=F32)


def _head(h):
    return slice(h * HEAD_DIM, (h + 1) * HEAD_DIM)


def _lane_chunk(c):
    return slice(c * LANES, (c + 1) * LANES)


def _round_bf16(x):
    bits = struct.unpack("<I", struct.pack("<f", x))[0]
    bits = (bits + 0x7FFF + ((bits >> 16) & 1)) & 0xFFFF0000
    return struct.unpack("<f", struct.pack("<I", bits))[0]


SLOPE_TERMS = 3
DENOM_ROWS = 16


def _bf16_split(x):
    parts = []
    for _ in range(SLOPE_TERMS):
        parts.append(_round_bf16(x))
        x = x - parts[-1]
    return tuple(parts)


def _tile_scale(scaled_tiles):
    if scaled_tiles is None or not any(scaled_tiles):
        return None
    if all(scaled_tiles):
        return SCALE2
    j = pl.program_id(1)
    is_scaled = functools.reduce(jnp.logical_or, [j == t for t, on in enumerate(scaled_tiles) if on])
    return jnp.where(is_scaled, SCALE2, 1.0)


def _mm_kernel(x_ref, w_ref, o_ref, *rest, scaled_tiles, transposed_tile, cast_rows):
    rest = list(rest)
    vt_ref = rest.pop(0) if transposed_tile is not None else None
    if cast_rows:
        xb_ref = rest.pop(0)

        @pl.when(pl.program_id(1) == 0)
        def _():
            xb_ref[...] = x_ref[...].astype(BF16)

        rows = xb_ref[...]
    else:
        rows = x_ref[...]
    acc = _dot(rows, w_ref[...].astype(BF16))
    scale = _tile_scale(scaled_tiles)
    o_ref[...] = (acc if scale is None else acc * scale).astype(o_ref.dtype)
    if vt_ref is not None:
        @pl.when(pl.program_id(1) == transposed_tile)
        def _():
            vt_ref[0] = acc.T.astype(vt_ref.dtype)


def _matmul(x, w, layer, tm, tn, name, query_cols=(), transposed_cols=None, seq_len=None):
    m, k = x.shape
    n = w.shape[2]
    is_f32 = x.dtype == F32
    scaled_tiles = tuple(any(lo <= j * tn and (j + 1) * tn <= hi for lo, hi in query_cols)
                         for j in range(n // tn))
    assert sum(scaled_tiles) * tn == sum(hi - lo for lo, hi in query_cols)
    out_specs = [pl.BlockSpec((tm, tn), lambda i, j: (i, j))]
    out_shape = [jax.ShapeDtypeStruct((m, n), BF16)]
    transposed_tile = None
    if transposed_cols is not None:
        lo, hi = transposed_cols
        assert hi - lo == tn and lo % tn == 0 and seq_len % tm == 0
        transposed_tile = lo // tn
        per_seq = seq_len // tm
        out_specs.append(pl.BlockSpec((1, tn, tm), lambda i, j: (i // per_seq, 0, i % per_seq)))
        out_shape.append(jax.ShapeDtypeStruct((m // seq_len, tn, seq_len), BF16))
    body = functools.partial(_mm_kernel, scaled_tiles=scaled_tiles, transposed_tile=transposed_tile,
                             cast_rows=is_f32)
    outs = pl.pallas_call(
        body,
        grid=(m // tm, n // tn),
        in_specs=[pl.BlockSpec((tm, k), lambda i, j: (i, 0)),
                  pl.BlockSpec((None, k, tn), lambda i, j: (layer, 0, j))],
        out_specs=out_specs,
        out_shape=out_shape,
        scratch_shapes=[pltpu.VMEM((tm, k), BF16)] if is_f32 else [],
        compiler_params=_params(("parallel", "arbitrary"), VMEM_LIMIT_BIG_TILES),
        name=name,
    )(x, w)
    return outs if transposed_cols is not None else outs[0]


def _matmul_all_layers(x, w, tn, name):
    m, k = x.shape
    nl, _, n = w.shape
    return pl.pallas_call(
        functools.partial(_mm_kernel, scaled_tiles=None, transposed_tile=None, cast_rows=True),
        grid=(nl, n // tn),
        in_specs=[pl.BlockSpec((m, k), lambda l, j: (0, 0)),
                  pl.BlockSpec((None, k, tn), lambda l, j: (l, 0, j))],
        out_specs=pl.BlockSpec((None, m, tn), lambda l, j: (l, 0, j)),
        out_shape=jax.ShapeDtypeStruct((nl, m, n), BF16),
        scratch_shapes=[pltpu.VMEM((m, k), BF16)],
        compiler_params=_params(("arbitrary", "arbitrary")),
        name=name,
    )(x, w)


def _kv_kernel(x_ref, w_ref, kv_ref, km_ref):
    acc = _dot(x_ref[...].astype(BF16), w_ref[...].astype(BF16))
    kv_ref[...] = acc.astype(kv_ref.dtype)
    nblk = acc.shape[0] // MOBA_BLOCK
    k = acc[:, :KV_W].reshape(nblk, MOBA_BLOCK, KV_W)
    km_ref[...] = jnp.mean(k, axis=1)[:, None, :]


def _shared_kv(x, w, tm):
    m, k = x.shape
    n = w.shape[1]
    nblk = tm // MOBA_BLOCK
    return pl.pallas_call(
        _kv_kernel,
        grid=(m // tm,),
        in_specs=[pl.BlockSpec((tm, k), lambda i: (i, 0)),
                  pl.BlockSpec((k, n), lambda i: (0, 0))],
        out_specs=[pl.BlockSpec((tm, n), lambda i: (i, 0)),
                   pl.BlockSpec((nblk, 1, KV_W), lambda i: (i, 0, 0))],
        out_shape=[jax.ShapeDtypeStruct((m, n), BF16),
                   jax.ShapeDtypeStruct((m // MOBA_BLOCK, 1, KV_W), F32)],
        compiler_params=_params(("parallel",)),
        name="shared_kv",
    )(x, w)


def _layer_norm(r, g, b):
    mu = jnp.mean(r, axis=-1, keepdims=True)
    c = r - mu
    var = jnp.mean(c * c, axis=-1, keepdims=True)
    return c * lax.rsqrt(var + LN_EPS) * g + b


def _ones_rows(n_keys):
    return jnp.where(lax.broadcasted_iota(jnp.int32, (DENOM_ROWS, n_keys), 0) == 0, 1.0, 0.0).astype(BF16)


def _memory_heads(qm_ref, mk_ref, mvt_ref, o_ref):
    tq = qm_ref.shape[1]
    step = min(tq, 2 * LANES)
    units = [(h, r0) for r0 in range(0, tq, step) for h in range(N_MEM_HEADS)]
    scores = [_dot_nt(mk_ref[0, :, _head(h)], qm_ref[0, r0:r0 + step, _head(h)])
              for h, r0 in units]
    probs = [jnp.exp2(s - jnp.max(s, axis=0, keepdims=True)).astype(BF16) for s in scores]
    ones_rows = _ones_rows(mvt_ref.shape[2])
    mvt_aug = [jnp.concatenate([mvt_ref[0, _head(h), :], ones_rows], axis=0) for h in range(N_MEM_HEADS)]
    outs = [_dot(mvt_aug[h], p) for (h, _), p in zip(units, probs)]
    for (h, r0), ot in zip(units, outs):
        ot = ot[:HEAD_DIM] / ot[HEAD_DIM:HEAD_DIM + 1]
        o_ref[0, r0:r0 + step, Q_W + h * HEAD_DIM:Q_W + (h + 1) * HEAD_DIM] = ot.T.astype(o_ref.dtype)


def _swa_kernel(sink_ref, q_ref, kp_ref, kc_ref, vtp_ref, vtc_ref, qm_ref, mk_ref, mvt_ref, o_ref):
    tq = q_ref.shape[1]
    n = pl.program_id(1)
    sw = WINDOW
    c_io = lax.broadcasted_iota(jnp.int32, (2 * sw, sw), 0)
    r_io = lax.broadcasted_iota(jnp.int32, (2 * sw, sw), 1)
    dist_i = r_io + sw - c_io
    band = (dist_i >= 0) & (dist_i < sw)
    neg_dist = jnp.where(band, -dist_i.astype(F32), NEG_INF)
    first_row = jnp.where(n > 0, 0, sw)
    neg_dist_first = jnp.where(c_io >= first_row, neg_dist, NEG_INF)
    for sb in range(tq // sw):
        rows = slice(sb * sw, (sb + 1) * sw)
        nd = neg_dist_first if sb == 0 else neg_dist
        scores, vtws = [], []
        for kvh in range(N_KV_HEADS):
            cs = _head(kvh)
            if sb == 0:
                kw = jnp.concatenate([kp_ref[0, :, cs], kc_ref[0, :sw, cs]], axis=0)
                vtw = jnp.concatenate([vtp_ref[0, cs, :], vtc_ref[0, cs, :sw]], axis=1)
            else:
                kw = kc_ref[0, (sb - 1) * sw:(sb + 1) * sw, cs]
                vtw = vtc_ref[0, cs, (sb - 1) * sw:(sb + 1) * sw]
            q = jnp.concatenate([q_ref[0, rows, _head(kvh * GROUP + g)] for g in range(GROUP)], axis=0)
            scores.append(_dot_nt(kw, q))
            vtws.append(vtw)
        probs, sink_terms = [], []
        for h in range(N_Q_HEADS):
            kvh, g = divmod(h, GROUP)
            s = scores[kvh][:, g * sw:(g + 1) * sw] + SLOPES2[h] * nd
            sink = sink_ref[h] * LOG2E
            m = jnp.maximum(jnp.max(s, axis=0, keepdims=True), sink)
            probs.append(jnp.exp2(s - m).astype(BF16))
            sink_terms.append(jnp.exp2(sink - m))
        ones_rows = _ones_rows(2 * sw)
        outs = [_dot(jnp.concatenate([vtws[kvh], ones_rows], axis=0),
                     jnp.concatenate(probs[kvh * GROUP:(kvh + 1) * GROUP], axis=1))
                for kvh in range(N_KV_HEADS)]
        for h in range(N_Q_HEADS):
            kvh, g = divmod(h, GROUP)
            ot = outs[kvh][:, g * sw:(g + 1) * sw]
            ot = ot[:HEAD_DIM] / (ot[HEAD_DIM:HEAD_DIM + 1] + sink_terms[h])
            o_ref[0, rows, _head(h)] = ot.T.astype(o_ref.dtype)
    _memory_heads(qm_ref, mk_ref, mvt_ref, o_ref)


def _swa_layer(proj, vt, mkv, mvt, sinks, tq):
    b, t, _ = proj.shape
    mlen = mkv.shape[1]
    wpb = tq // WINDOW
    kcol = Q_W // KV_W
    prev = lambda n: jnp.maximum(n * wpb - 1, 0)
    return pl.pallas_call(
        _swa_kernel,
        grid=(b, t // tq),
        in_specs=[
            pl.BlockSpec(memory_space=pltpu.SMEM),
            pl.BlockSpec((1, tq, Q_W), lambda bi, n: (bi, n, 0)),
            pl.BlockSpec((1, WINDOW, KV_W), lambda bi, n: (bi, prev(n), kcol)),
            pl.BlockSpec((1, tq, KV_W), lambda bi, n: (bi, n, kcol)),
            pl.BlockSpec((1, KV_W, WINDOW), lambda bi, n: (bi, 0, prev(n))),
            pl.BlockSpec((1, KV_W, tq), lambda bi, n: (bi, 0, n)),
            pl.BlockSpec((1, tq, MQ_W), lambda bi, n: (bi, n, kcol + 2)),
            pl.BlockSpec((1, mlen, MQ_W), lambda bi, n: (bi, 0, 0)),
            pl.BlockSpec((1, MQ_W, mlen), lambda bi, n: (bi, 0, 0)),
        ],
        out_specs=pl.BlockSpec((1, tq, MIX_W), lambda bi, n: (bi, n, 0)),
        out_shape=jax.ShapeDtypeStruct((b, t, MIX_W), BF16),
        compiler_params=_params(("parallel", "arbitrary")),
        name="swa_mem_attn",
    )(sinks, proj, proj, proj, vt, vt, proj, mkv, mvt)


def _moba_kernel(q_ref, k_ref, vt_ref, km_ref, qm_ref, mk_ref, mvt_ref, o_ref,
                 qa_ref, sel_ref, m_ref, acc_ref):
    tq = MOBA_BLOCK
    nb = k_ref.shape[1]
    per_head = tq // LANES
    j = pl.program_id(1)
    key = lax.broadcasted_iota(jnp.int32, (tq, tq), 0)
    qry = lax.broadcasted_iota(jnp.int32, (tq, tq), 1)
    causal_add = jnp.where(key <= qry, 0.0, NEG_INF)
    blk = lax.broadcasted_iota(jnp.int32, (nb, tq), 0)
    heads = range(N_Q_HEADS)

    lane_hd = lax.broadcasted_iota(jnp.int32, (tq, HEAD_DIM), 1)
    c_cols = jnp.where(lane_hd < SLOPE_TERMS, lax.broadcasted_iota(jnp.int32, (tq, HEAD_DIM), 0), 0).astype(BF16)
    for h in heads:
        slope_cols = sum(jnp.where(lane_hd == t, p, 0.0) for t, p in enumerate(_bf16_split(SLOPES2[h])))
        qa_ref[h] = jnp.concatenate([q_ref[0, :, _head(h)], slope_cols.astype(BF16)], axis=1)

    def head_lanes(h):
        return slice(h * tq, (h + 1) * tq)

    def scores_of(i):
        k_aug = [jnp.concatenate([k_ref[0, i, :, _head(kvh)], c_cols], axis=1) for kvh in range(N_KV_HEADS)]
        return [_dot_nt(k_aug[h // GROUP], qa_ref[h]) for h in heads]

    ones_rows = _ones_rows(tq)

    def values_of(i, probs):
        vt_aug = [jnp.concatenate([vt_ref[0, i, _head(kvh), :], ones_rows], axis=0) for kvh in range(N_KV_HEADS)]
        return [_dot(vt_aug[h // GROUP], probs[h]) for h in heads]

    for h in heads:
        kmb = km_ref[0, :, _head(h // GROUP)].astype(BF16)
        gate = jnp.where(blk < j, _dot_nt(kmb, q_ref[0, :, _head(h)]), NEG_INF)
        cnt = jnp.zeros((nb, tq), F32)
        for ip in range(nb):
            gi = gate[ip:ip + 1, :]
            beats = (gi > gate) | ((gi == gate) & (ip < blk))
            cnt = cnt + jnp.where(beats, 1.0, 0.0)
        sel = jnp.where((cnt < MOBA_TOPK) & (blk < j), 0.0, NEG_INF)
        for ip in range(nb):
            sel_ref[ip, :, head_lanes(h)] = sel[ip:ip + 1, :]
    slope_row = jnp.concatenate([jnp.full((1, tq), SLOPES2[h], F32) for h in heads], axis=1)

    raw = scores_of(j)
    probs = []
    for h in heads:
        ps = []
        for half in range(per_head):
            lanes, hl = _lane_chunk(h * per_head + half), _lane_chunk(half)
            s = raw[h][:, hl] + causal_add[:, hl]
            m0 = jnp.max(s, axis=0, keepdims=True)
            p = jnp.exp2(s - m0)
            m_ref[:, lanes] = m0
            ps.append(p.astype(BF16))
        probs.append(jnp.concatenate(ps, axis=1))
    for h, pv in zip(heads, values_of(j, probs)):
        acc_ref[:, head_lanes(h)] = pv

    def past_block(i, carry):
        gap = ((j - i) * tq).astype(F32)
        rowc = sel_ref[i] - slope_row * gap
        raw = scores_of(i)
        probs, scales = [], []
        for h in heads:
            ps, sc = [], []
            for half in range(per_head):
                lanes, hl = _lane_chunk(h * per_head + half), _lane_chunk(half)
                s = raw[h][:, hl]
                m_old = m_ref[:, lanes]
                m_new = jnp.maximum(m_old, jnp.max(s, axis=0, keepdims=True) + rowc[:, lanes])
                a = jnp.exp2(m_old - m_new)
                p = jnp.exp2(s - (m_new - rowc[:, lanes]))
                m_ref[:, lanes] = m_new
                ps.append(p.astype(BF16))
                sc.append(a)
            probs.append(jnp.concatenate(ps, axis=1))
            scales.append(jnp.concatenate(sc, axis=1))
        for h, pv in zip(heads, values_of(i, probs)):
            acc_ref[:, head_lanes(h)] = scales[h] * acc_ref[:, head_lanes(h)] + pv
        return carry

    lax.fori_loop(0, j, past_block, 0)
    for h in heads:
        ot = acc_ref[:HEAD_DIM, head_lanes(h)] / acc_ref[HEAD_DIM:HEAD_DIM + 1, head_lanes(h)]
        o_ref[0, :, _head(h)] = ot.T.astype(o_ref.dtype)
    _memory_heads(qm_ref, mk_ref, mvt_ref, o_ref)


def _moba_layer(proj, k4, vt4, kmeans, mkv, mvt):
    b, t, _ = proj.shape
    mlen = mkv.shape[1]
    tq = MOBA_BLOCK
    nb = t // tq
    hq = N_Q_HEADS * tq
    return pl.pallas_call(
        _moba_kernel,
        grid=(b, nb),
        in_specs=[
            pl.BlockSpec((1, tq, Q_W), lambda bi, n: (bi, n, 0)),
            pl.BlockSpec((1, nb, tq, KV_W), lambda bi, n: (bi, 0, 0, 0)),
            pl.BlockSpec((1, nb, KV_W, tq), lambda bi, n: (bi, 0, 0, 0)),
            pl.BlockSpec((1, nb, KV_W), lambda bi, n: (bi, 0, 0)),
            pl.BlockSpec((1, tq, MQ_W), lambda bi, n: (bi, n, Q_W // MQ_W)),
            pl.BlockSpec((1, mlen, MQ_W), lambda bi, n: (bi, 0, 0)),
            pl.BlockSpec((1, MQ_W, mlen), lambda bi, n: (bi, 0, 0)),
        ],
        out_specs=pl.BlockSpec((1, tq, MIX_W), lambda bi, n: (bi, n, 0)),
        out_shape=jax.ShapeDtypeStruct((b, t, MIX_W), BF16),
        scratch_shapes=[pltpu.VMEM((N_Q_HEADS, tq, 2 * HEAD_DIM), BF16),
                        pltpu.VMEM((nb, 1, hq), F32),
                        pltpu.VMEM((1, hq), F32),
                        pltpu.VMEM((HEAD_DIM + DENOM_ROWS, hq), F32)],
        compiler_params=_params(("parallel", "arbitrary")),
        name="moba_mem_attn",
    )(proj, k4, vt4, kmeans, proj, mkv, mvt)


OUT_ROW_CHUNK = 256


def _out_proj_kernel(mix_ref, w_ref, h_ref, g_ref, b_ref, o_ref, wb_ref):
    @pl.when(pl.program_id(0) == 0)
    def _():
        wb_ref[...] = w_ref[...].astype(BF16)

    for r0 in range(0, o_ref.shape[0], OUT_ROW_CHUNK):
        rows = slice(r0, r0 + OUT_ROW_CHUNK)
        r = ALPHA * h_ref[rows, :] + _dot(mix_ref[rows, :], wb_ref[...])
        o_ref[rows, :] = _layer_norm(r, g_ref[...], b_ref[...])


def _ln_specs(ln_index):
    spec = pl.BlockSpec((None, 1, D_MODEL), lambda *_: (ln_index, 0, 0))
    return [spec, spec]


def _out_proj_ln(mix, w, layer, h, ln_g, ln_b, tm):
    m, k = mix.shape
    n = w.shape[2]
    return pl.pallas_call(
        _out_proj_kernel,
        grid=(m // tm,),
        in_specs=[pl.BlockSpec((tm, k), lambda i: (i, 0)),
                  pl.BlockSpec((None, k, n), lambda i: (layer, 0, 0), pipeline_mode=pl.Buffered(1)),
                  pl.BlockSpec((tm, n), lambda i: (i, 0))] + _ln_specs(2 * layer),
        out_specs=pl.BlockSpec((tm, n), lambda i: (i, 0)),
        out_shape=jax.ShapeDtypeStruct((m, n), F32),
        scratch_shapes=[pltpu.VMEM((k, n), BF16)],
        compiler_params=_params(("arbitrary",)),
        name="out_proj_ln",
    )(mix, w, h, ln_g, ln_b)


def _ffn_kernel(h_ref, wu_ref, wd_ref, g_ref, b_ref, o_ref, ob_ref):
    f = pl.program_id(1)

    @pl.when(f == 0)
    def _():
        ob_ref[...] = h_ref[...].astype(BF16)
        o_ref[...] = ALPHA * h_ref[...]

    u = jnp.maximum(_dot(ob_ref[...], wu_ref[...].astype(BF16)), 0.0)
    o_ref[...] += _dot((u * u).astype(BF16), wd_ref[...].astype(BF16))

    @pl.when(f == pl.num_programs(1) - 1)
    def _():
        y = _layer_norm(o_ref[...], g_ref[...], b_ref[...])
        o_ref[...] = y
        ob_ref[...] = y.astype(BF16)


def _ffn_ln(h, w_up, w_down, layer, ln_g, ln_b, tm, tf, emit_bf16):
    m, d = h.shape
    dff = w_up.shape[2]
    row_spec = pl.BlockSpec((tm, d), lambda i, f: (i, 0))
    outs = pl.pallas_call(
        _ffn_kernel,
        grid=(m // tm, dff // tf),
        in_specs=[row_spec,
                  pl.BlockSpec((None, d, tf), lambda i, f: (layer, 0, f)),
                  pl.BlockSpec((None, tf, d), lambda i, f: (layer, f, 0))] + _ln_specs(2 * layer + 1),
        out_specs=[row_spec, row_spec] if emit_bf16 else [row_spec],
        out_shape=[jax.ShapeDtypeStruct((m, d), F32)] + ([jax.ShapeDtypeStruct((m, d), BF16)] if emit_bf16 else []),
        scratch_shapes=[] if emit_bf16 else [pltpu.VMEM((tm, d), BF16)],
        compiler_params=_params(("parallel", "arbitrary"), VMEM_LIMIT_BIG_TILES),
        name="ffn_ln",
    )(h, w_up, w_down, ln_g, ln_b)
    return (outs[0], outs[1]) if emit_bf16 else (outs[0], None)


PROJ_TM_F32, PROJ_TM_BF16 = 2048, 2048
PROJ_A_TN, PROJ_B_TN = 512, 1024
KV_TM = 1024
OUT_TM = 512
FFN_TM, FFN_TF = 1024, 512
SWA_TQ = 1024


def kernel(x, mem, w_in_a, sinks_a, w_q_b, w_kv_shared, w_mem_kv, w_o, w_up, w_down, ln_g, ln_b):
    b, t, d = x.shape
    mlen = mem.shape[1]
    m = b * t
    nb = t // MOBA_BLOCK
    h = x.reshape(m, d)
    mem2 = mem.reshape(b * mlen, d)
    ln_g2 = ln_g.reshape(2 * DEPTH, 1, d)
    ln_b2 = ln_b.reshape(2 * DEPTH, 1, d)
    mkv_all = _matmul_all_layers(mem2, w_mem_kv, 512, "mem_kv").reshape(DEPTH, b, mlen, 2 * MQ_W)
    mvt_all = jnp.swapaxes(mkv_all[..., MQ_W:], 2, 3)
    k4 = vt4 = kmeans = None
    hx = h
    for layer in range(DEPTH):
        proj_tm = PROJ_TM_F32 if hx.dtype == F32 else PROJ_TM_BF16
        mkv, mvt = mkv_all[layer], mvt_all[layer]
        if layer < N_A_LAYERS:
            proj, vt = _matmul(hx, w_in_a, layer, proj_tm, PROJ_A_TN, "proj_a",
                               query_cols=((0, Q_W), (Q_W + 2 * KV_W, Q_W + 2 * KV_W + MQ_W)),
                               transposed_cols=(Q_W + KV_W, Q_W + 2 * KV_W), seq_len=t)
            proj = proj.reshape(b, t, -1)
            mix = _swa_layer(proj, vt, mkv, mvt, sinks_a[layer], SWA_TQ)
        else:
            if k4 is None:
                kv, km = _shared_kv(hx, w_kv_shared, KV_TM)
                k4 = kv.reshape(b, nb, MOBA_BLOCK, 2 * KV_W)
                vt4 = jnp.swapaxes(k4[..., KV_W:], 2, 3)
                kmeans = km.reshape(b, nb, KV_W)
            proj = _matmul(hx, w_q_b, layer - N_A_LAYERS, proj_tm, PROJ_B_TN, "proj_b", query_cols=((0, MIX_W),))
            mix = _moba_layer(proj.reshape(b, t, -1), k4, vt4, kmeans, mkv, mvt)
        h = _out_proj_ln(mix.reshape(m, MIX_W), w_o, layer, h, ln_g2, ln_b2, OUT_TM)
        h, hx = _ffn_ln(h, w_up, w_down, layer, ln_g2, ln_b2, FFN_TM, FFN_TF, emit_bf16=layer + 1 < DEPTH)
    return h.reshape(b, t, d)
```
